```python
import jax, jax.numpy as jnp
from jax import lax
import numpy as np

D_MODEL = 1024
BATCH = 32
SEQ = 2048
DEPTH = 1

CHUNK = 64
Q_BLOCK = 128
EPS = 1e-6
CONV_WIDTH = D_MODEL
CONV_KERNEL = 31
N_HEADS = 16
HEAD_DIM = 64
V_DIM = 64
Q_LORA = 256
KV_LORA = 128
IDX_HEADS = 8
IDX_DIM = 64
IDX_TOPK_MAX = 256
D_FF = 4 * D_MODEL
N_BRANCH = 2
IN_SIZES = (2 * CONV_WIDTH, Q_LORA, KV_LORA, IDX_DIM, IDX_HEADS, N_BRANCH * D_MODEL)
D_IN = sum(IN_SIZES)

kernel_name = "hybrid_conformer_dsa_block"


def rms_norm(x, g):
    xf = x.astype(jnp.float32)
    y = xf * lax.rsqrt(jnp.mean(xf * xf, axis=-1, keepdims=True) + EPS)
    return (y * g.astype(jnp.float32)).astype(x.dtype)


def layer_norm(x, g, b):
    xf = x.astype(jnp.float32)
    mu = jnp.mean(xf, axis=-1, keepdims=True)
    var = jnp.mean(jnp.square(xf - mu), axis=-1, keepdims=True)
    y = (xf - mu) * lax.rsqrt(var + EPS)
    return (y * g.astype(jnp.float32) + b.astype(jnp.float32)).astype(x.dtype)


def split_columns(u):
    offs = np.cumsum(IN_SIZES)[:-1].tolist()
    return jnp.split(u, offs, axis=-1)


def conv_branch(u_conv, dw_w, dw_b, ln_g, ln_b, w_pw):
    a, gate = jnp.split(u_conv, 2, axis=-1)
    v = a * jax.nn.sigmoid(gate)
    y = lax.conv_general_dilated(
        v, dw_w[:, None, :].astype(v.dtype), window_strides=(1,),
        padding=[(CONV_KERNEL - 1, 0)],
        dimension_numbers=("NWC", "WIO", "NWC"),
        feature_group_count=CONV_WIDTH)
    y = layer_norm(y + dw_b, ln_g, ln_b)
    return jax.nn.silu(y) @ w_pw


def sparse_attention_branch(c_q, c_kv, k_idx_raw, w_idx_raw, q_norm_g, kv_norm_g,
                            w_uq, w_uk, w_uv, w_qi, kidx_ln_g, kidx_ln_b, w_attn_out):
    B, L, _ = c_q.shape
    c_q = rms_norm(c_q, q_norm_g)
    c_kv = rms_norm(c_kv, kv_norm_g)
    q = (c_q @ w_uq).reshape(B, L, N_HEADS, HEAD_DIM)
    q_lat = jnp.einsum("blhd,hdc->blhc", q, w_uk)
    q_idx = (c_q @ w_qi).reshape(B, L, IDX_HEADS, IDX_DIM).astype(jnp.float32)
    k_idx = layer_norm(k_idx_raw, kidx_ln_g, kidx_ln_b).astype(jnp.float32)
    w_idx = w_idx_raw.astype(jnp.float32) * (IDX_HEADS ** -0.5 * IDX_DIM ** -0.5)
    top_k = min(IDX_TOPK_MAX, L // 4)
    key_chunk = jnp.arange(L) // CHUNK

    def block(i):
        start = i * Q_BLOCK
        qi = lax.dynamic_slice_in_dim(q_idx, start, Q_BLOCK, axis=1)
        wi = lax.dynamic_slice_in_dim(w_idx, start, Q_BLOCK, axis=1)
        ql = lax.dynamic_slice_in_dim(q_lat, start, Q_BLOCK, axis=1)
        q_chunk = (start + jnp.arange(Q_BLOCK)) // CHUNK
        admissible = key_chunk[None, :] <= q_chunk[:, None]
        logits = jnp.einsum("bqhd,bsd->bqhs", qi, k_idx)
        score = jnp.einsum("bqh,bqhs->bqs", wi, jax.nn.relu(logits))
        score = jnp.where(admissible[None], score, -jnp.inf)
        _, sel = lax.top_k(score, top_k)
        valid = key_chunk[sel] <= q_chunk[None, :, None]
        c_sel = jax.vmap(lambda c, s: c[s])(c_kv, sel)
        s = jnp.einsum("bqhc,bqkc->bqhk", ql, c_sel).astype(jnp.float32) * (HEAD_DIM ** -0.5)
        s = jnp.where(valid[:, :, None, :], s, -jnp.inf)
        p = jax.nn.softmax(s, axis=-1).astype(c_sel.dtype)
        o_lat = jnp.einsum("bqhk,bqkc->bqhc", p, c_sel)
        o = jnp.einsum("bqhc,hcv->bqhv", o_lat, w_uv).reshape(B, Q_BLOCK, N_HEADS * V_DIM)
        return o @ w_attn_out

    out = lax.map(block, jnp.arange(L // Q_BLOCK))
    return jnp.moveaxis(out, 0, 1).reshape(B, L, D_MODEL)


def setup_inputs(seed: int = 0) -> dict:
    key = jax.random.key(seed)
    ks = jax.random.split(key, 24)
    f = jnp.float32

    def nrm(k, shape, scale):
        return jax.random.normal(k, shape, f) * scale

    def gain(k, shape):
        return 1.0 + 0.02 * jax.random.normal(k, shape, f)

    Ld = DEPTH
    return {
        "x": jax.random.normal(ks[0], (BATCH, SEQ, D_MODEL), f),
        "attn_norm_g": gain(ks[1], (Ld, D_MODEL)),
        "w_in": nrm(ks[2], (Ld, D_MODEL, D_IN), D_MODEL ** -0.5),
        "b_gate": nrm(ks[3], (Ld, N_BRANCH * D_MODEL), 0.1),
        "dw_w": nrm(ks[4], (Ld, CONV_KERNEL, CONV_WIDTH), CONV_KERNEL ** -0.5),
        "dw_b": nrm(ks[5], (Ld, CONV_WIDTH), 0.02),
        "conv_ln_g": gain(ks[6], (Ld, CONV_WIDTH)),
        "conv_ln_b": nrm(ks[7], (Ld, CONV_WIDTH), 0.02),
        "w_conv_out": nrm(ks[8], (Ld, CONV_WIDTH, D_MODEL), CONV_WIDTH ** -0.5),
        "q_norm_g": gain(ks[9], (Ld, Q_LORA)),
        "kv_norm_g": gain(ks[10], (Ld, KV_LORA)),
        "w_uq": nrm(ks[11], (Ld, Q_LORA, N_HEADS * HEAD_DIM), Q_LORA ** -0.5),
        "w_uk": nrm(ks[12], (Ld, N_HEADS, HEAD_DIM, KV_LORA), KV_LORA ** -0.5),
        "w_uv": nrm(ks[13], (Ld, N_HEADS, KV_LORA, V_DIM), KV_LORA ** -0.5),
        "w_qi": nrm(ks[14], (Ld, Q_LORA, IDX_HEADS * IDX_DIM), Q_LORA ** -0.5),
        "kidx_ln_g": gain(ks[15], (Ld, IDX_DIM)),
        "kidx_ln_b": nrm(ks[16], (Ld, IDX_DIM), 0.02),
        "w_attn_out": nrm(ks[17], (Ld, N_HEADS * V_DIM, D_MODEL), (N_HEADS * V_DIM) ** -0.5),
        "w_o": nrm(ks[18], (Ld, D_MODEL, D_MODEL), D_MODEL ** -0.5),
        "mlp_norm_g": gain(ks[19], (Ld, D_MODEL)),
        "w_ff1": nrm(ks[20], (Ld, D_MODEL, D_FF), D_MODEL ** -0.5),
        "w_ff2": nrm(ks[21], (Ld, D_FF, D_MODEL), D_FF ** -0.5),
        "final_norm_g": gain(ks[22], (D_MODEL,)),
    }


def reference(x, attn_norm_g, w_in, b_gate, dw_w, dw_b, conv_ln_g, conv_ln_b, w_conv_out,
              q_norm_g, kv_norm_g, w_uq, w_uk, w_uv, w_qi, kidx_ln_g, kidx_ln_b, w_attn_out,
              w_o, mlp_norm_g, w_ff1, w_ff2, final_norm_g):
    B, L, _ = x.shape
    for i in range(DEPTH):
        h = rms_norm(x, attn_norm_g[i])
        u = h @ w_in[i]
        u_conv, c_q, c_kv, k_idx_raw, w_idx_raw, u_gate = split_columns(u)
        y_a = conv_branch(u_conv, dw_w[i], dw_b[i], conv_ln_g[i], conv_ln_b[i], w_conv_out[i])
        y_b = sparse_attention_branch(c_q, c_kv, k_idx_raw, w_idx_raw, q_norm_g[i], kv_norm_g[i],
                                      w_uq[i], w_uk[i], w_uv[i], w_qi[i], kidx_ln_g[i],
                                      kidx_ln_b[i], w_attn_out[i])
        gates = jax.nn.sigmoid(u_gate + b_gate[i]).reshape(B, L, N_BRANCH, D_MODEL)
        merged = gates[:, :, 0, :] * y_a + gates[:, :, 1, :] * y_b
        x = x + merged @ w_o[i]
        hm = rms_norm(x, mlp_norm_g[i])
        x = x + jnp.square(jax.nn.relu(hm @ w_ff1[i])) @ w_ff2[i]
    return rms_norm(x, final_norm_g)
```

```python
import functools

import jax
import jax.numpy as jnp
from jax import lax
from jax.experimental import pallas as pl
from jax.experimental.pallas import tpu as pltpu

D_MODEL = 1024
CHUNK = 64
Q_BLOCK = 128
EPS = 1e-6
CONV_KERNEL = 31
N_HEADS = 16
HEAD_DIM = 64
V_DIM = 64
Q_LORA = 256
KV_LORA = 128
IDX_HEADS = 8
IDX_DIM = 64
IDX_TOPK_MAX = 256
D_FF = 4 * D_MODEL

V7X_LANES = 128
V7X_SUBLANES = 8
V7X_VMEM_LIMIT_BYTES = 60000 * 1024

CONV_HALO = 32
CONV_ROWS = 16
KEY_STEP = 2 * Q_BLOCK
NEG_BIG = -1e30
INT_MIN = -(2 ** 31)

F32 = jnp.float32
BF16 = jnp.bfloat16
I32 = jnp.int32


def _dot(a, b):
    return jnp.dot(a, b, preferred_element_type=F32)


def _dot_nt(a, b):
    return lax.dot_general(a, b, (((1,), (1,)), ((), ())), preferred_element_type=F32)


def _rms(x, g):
    ms = jnp.mean(x * x, axis=-1, keepdims=True)
    return x * lax.rsqrt(ms + EPS) * g


def _sigmoid(x):
    return 1.0 / (1.0 + jnp.exp(-x))


def _kernel_a(x_ref, ng_ref, wbig_ref, wsm_ref, bgate_ref, dww_ref, dwb_ref, lng_ref, lnb_ref,
              wpw_ref, qng_ref, kvng_ref, wuq_ref, wukbd_ref, wqi_ref, klng_ref, klnb_ref,
              ma_ref, gb_ref, qlat_ref, qidx_ref, small_ref,
              vbuf, zbuf, *, tm, tiles_per_seq):
    t = pl.program_id(0)
    first = (t % tiles_per_seq) == 0

    x = x_ref[...]
    h = _rms(x, ng_ref[...]).astype(BF16)

    ua = _dot(h, wbig_ref[:, 0:D_MODEL])
    ug = _dot(h, wbig_ref[:, D_MODEL:2 * D_MODEL])
    v = ua * _sigmoid(ug)

    @pl.when(first)
    def _():
        vbuf[0:CONV_HALO, :] = jnp.zeros((CONV_HALO, D_MODEL), F32)

    @pl.when(jnp.logical_not(first))
    def _():
        vbuf[0:CONV_HALO, :] = vbuf[tm:tm + CONV_HALO, :]

    vbuf[CONV_HALO:CONV_HALO + tm, :] = v

    tap0 = CONV_HALO - (CONV_KERNEL - 1)

    def conv_chunk(r, carry):
        base = pl.multiple_of(r * CONV_ROWS, CONV_ROWS)
        win = vbuf[pl.ds(base, CONV_ROWS + CONV_HALO), :]
        acc = jnp.zeros((CONV_ROWS, D_MODEL), F32)
        for k in range(CONV_KERNEL):
            acc = acc + dww_ref[k:k + 1, :] * win[tap0 + k:tap0 + k + CONV_ROWS, :]
        y = acc + dwb_ref[...]
        mu = jnp.mean(y, axis=-1, keepdims=True)
        yc = y - mu
        var = jnp.mean(yc * yc, axis=-1, keepdims=True)
        z = yc * lax.rsqrt(var + EPS) * lng_ref[...] + lnb_ref[...]
        z = z * _sigmoid(z)
        zbuf[pl.ds(base, CONV_ROWS), :] = z.astype(BF16)
        return carry

    lax.fori_loop(0, tm // CONV_ROWS, conv_chunk, 0)

    ya = _dot(zbuf[...], wpw_ref[...])
    ga = _sigmoid(_dot(h, wbig_ref[:, 2 * D_MODEL:3 * D_MODEL]) + bgate_ref[:, 0:D_MODEL])
    ma_ref[...] = (ga * ya).astype(BF16)
    gb = _sigmoid(_dot(h, wbig_ref[:, 3 * D_MODEL:4 * D_MODEL]) + bgate_ref[:, D_MODEL:2 * D_MODEL])
    gb_ref[...] = gb.astype(BF16)

    us = _dot(h, wsm_ref[...])
    cq = _rms(us[:, 0:Q_LORA], qng_ref[...]).astype(BF16)
    small_ref[:, 0:KV_LORA] = _rms(us[:, Q_LORA:Q_LORA + KV_LORA], kvng_ref[...])

    t3 = us[:, Q_LORA + KV_LORA:Q_LORA + KV_LORA + V7X_LANES]
    lane = lax.broadcasted_iota(I32, t3.shape, 1)
    is_k = lane < IDX_DIM
    mu = jnp.sum(jnp.where(is_k, t3, 0.0), axis=-1, keepdims=True) * (1.0 / IDX_DIM)
    kc = jnp.where(is_k, t3 - mu, 0.0)
    var = jnp.sum(kc * kc, axis=-1, keepdims=True) * (1.0 / IDX_DIM)
    kn = kc * lax.rsqrt(var + EPS) * klng_ref[...] + klnb_ref[...]
    wi = t3 * (IDX_HEADS ** -0.5 * IDX_DIM ** -0.5)
    small_ref[:, KV_LORA:KV_LORA + V7X_LANES] = jnp.where(
        is_k, kn, jnp.where(lane < IDX_DIM + IDX_HEADS, wi, 0.0))

    q = _dot(cq, wuq_ref[...]).astype(BF16)
    heads_per_group = wukbd_ref.shape[1] // HEAD_DIM
    lat_per_group = heads_per_group * KV_LORA
    for g in range(N_HEADS // heads_per_group):
        ql = _dot(q[:, g * heads_per_group * HEAD_DIM:(g + 1) * heads_per_group * HEAD_DIM],
                  wukbd_ref[g])
        qlat_ref[:, g * lat_per_group:(g + 1) * lat_per_group] = (ql * (HEAD_DIM ** -0.5)).astype(BF16)
    qidx_ref[...] = _dot(cq, wqi_ref[...]).astype(BF16)


def _kernel_b(kidx_ref, ckv_ref, ckvt_ref, widx_ref, qidx_ref, qlat_ref, wuvt_ref, o_ref,
              qall_scr, qlat_scr, key_scr, bias_scr, ot_scr, *, seq_len, top_k):
    i = pl.program_id(1)
    n_q = Q_BLOCK
    n_chunk128 = i + 1
    n_step = (i + 2) // 2

    lane = lax.broadcasted_iota(I32, (n_q, V7X_LANES), 1)
    for hh in range(IDX_HEADS):
        tile = qidx_ref[:, (hh // 2) * V7X_LANES:(hh // 2 + 1) * V7X_LANES].astype(F32)
        keep = (lane < IDX_DIM) if hh % 2 == 0 else (lane >= IDX_DIM)
        qall_scr[hh * n_q:(hh + 1) * n_q, :] = jnp.where(keep, tile, 0.0).astype(BF16)
    for hh in range(N_HEADS):
        qlat_scr[hh * n_q:(hh + 1) * n_q, :] = qlat_ref[:, hh * KV_LORA:(hh + 1) * KV_LORA]

    qabs_chunk = (i * n_q + lax.broadcasted_iota(I32, (Q_BLOCK, n_q), 1)) // CHUNK

    def score_chunk(kc, carry):
        ks = pl.multiple_of(kc * Q_BLOCK, Q_BLOCK)
        lg = _dot_nt(kidx_ref[pl.ds(ks, Q_BLOCK), :], qall_scr[...])
        sc = jnp.zeros((Q_BLOCK, n_q), F32)
        for hh in range(IDX_HEADS):
            sc = sc + widx_ref[hh:hh + 1, :] * jnp.maximum(lg[:, hh * n_q:(hh + 1) * n_q], 0.0)
        bits = pltpu.bitcast(sc, I32)
        key = bits ^ ((bits >> 31) & 0x7FFFFFFF)
        key = jnp.where(bits == INT_MIN, 0, key)
        kabs_chunk = (ks + lax.broadcasted_iota(I32, (Q_BLOCK, n_q), 0)) // CHUNK
        key_scr[pl.ds(ks, Q_BLOCK), :] = jnp.where(kabs_chunk <= qabs_chunk, key, INT_MIN)
        return carry

    lax.fori_loop(0, n_chunk128, score_chunk, 0)

    @pl.when(n_chunk128 % 2 == 1)
    def _():
        pad0 = pl.multiple_of(n_chunk128 * Q_BLOCK, Q_BLOCK)
        key_scr[pl.ds(pad0, Q_BLOCK), :] = jnp.full((Q_BLOCK, n_q), INT_MIN, I32)

    def count_rows(pred):
        def body(c, cnt):
            r0 = pl.multiple_of(c * KEY_STEP, KEY_STEP)
            hit = pred(key_scr[pl.ds(r0, KEY_STEP), :], r0)
            ones = jnp.where(hit, 1, 0).astype(I32)
            return cnt + jnp.sum(ones.reshape(KEY_STEP // V7X_SUBLANES, V7X_SUBLANES, n_q), axis=0)
        cnt8 = lax.fori_loop(0, n_step, body, jnp.zeros((V7X_SUBLANES, n_q), I32))
        return jnp.sum(cnt8, axis=0, keepdims=True)

    def bisect_value(it, cur):
        cand = cur ^ jnp.left_shift(jnp.int32(1), 31 - it)
        cnt = count_rows(lambda k, r0: k >= cand)
        return jnp.where(cnt >= top_k, cand, cur)

    thr = lax.fori_loop(0, 32, bisect_value, jnp.full((1, n_q), INT_MIN, I32))
    thr = jnp.maximum(thr, INT_MIN + 1)
    n_gt = count_rows(lambda k, r0: k > thr)
    n_ge = count_rows(lambda k, r0: k >= thr)
    need = top_k - n_gt

    def tie_limit():
        n_bits = max(1, (seq_len - 1).bit_length())

        def bisect_row(it, cur):
            cand = cur + jnp.left_shift(jnp.int32(1), n_bits - 1 - it)

            def pred(k, r0):
                row = r0 + lax.broadcasted_iota(I32, (KEY_STEP, n_q), 0)
                return jnp.logical_and(k == thr, row < cand)
            cnt = count_rows(pred)
            return jnp.where(cnt < need, cand, cur)

        return lax.fori_loop(0, n_bits, bisect_row, jnp.zeros((1, n_q), I32))

    any_cut = jnp.max(n_ge.astype(F32)) > top_k
    row_lim = lax.cond(any_cut, tie_limit, lambda: jnp.full((1, n_q), seq_len, I32))

    def bias_chunk(c, carry):
        r0 = pl.multiple_of(c * KEY_STEP, KEY_STEP)
        k = key_scr[pl.ds(r0, KEY_STEP), :]
        row = r0 + lax.broadcasted_iota(I32, (KEY_STEP, n_q), 0)
        take = jnp.logical_or(k > thr, jnp.logical_and(k == thr, row <= row_lim))
        bias_scr[pl.ds(r0, KEY_STEP), :] = jnp.where(take, 0.0, NEG_BIG).astype(F32)
        return carry

    lax.fori_loop(0, n_step, bias_chunk, 0)

    pair = 2 * n_q

    def head_pair(hp, carry):
        q2 = qlat_scr[pl.ds(pl.multiple_of(hp * pair, pair), pair), :]

        def attn_step(c, st):
            m, l, acc = st
            r0 = pl.multiple_of(c * KEY_STEP, KEY_STEP)
            s = _dot_nt(ckv_ref[pl.ds(r0, KEY_STEP), :], q2)
            b = bias_scr[pl.ds(r0, KEY_STEP), :]
            s = s + jnp.concatenate([b, b], axis=1)
            m_new = jnp.maximum(m, jnp.max(s, axis=0, keepdims=True))
            alpha = jnp.exp(m - m_new)
            p = jnp.exp(s - m_new)
            l = alpha * l + jnp.sum(p, axis=0, keepdims=True)
            pv = _dot(ckvt_ref[c], p.astype(BF16))
            return m_new, l, acc * alpha + pv

        init = (jnp.full((1, pair), NEG_BIG, F32), jnp.zeros((1, pair), F32),
                jnp.zeros((KV_LORA, pair), F32))
        _, l, acc = lax.fori_loop(0, n_step, attn_step, init)
        olat = (acc / l).astype(BF16)
        o0 = _dot(wuvt_ref[2 * hp], olat[:, 0:n_q])
        o1 = _dot(wuvt_ref[2 * hp + 1], olat[:, n_q:pair])
        ot_scr[pl.ds(pl.multiple_of(hp * 2 * V_DIM, 2 * V_DIM), 2 * V_DIM), :] = (
            jnp.concatenate([o0, o1], axis=0))
        return carry

    lax.fori_loop(0, N_HEADS // 2, head_pair, 0)
    o_ref[...] = ot_scr[...].T.astype(BF16)


def _kernel_c(x_ref, o_ref, ma_ref, gb_ref, wao_ref, wo_ref, mng_ref, w1_ref, w2_ref, fng_ref,
              out_ref, *, ff_step):
    yb = _dot(o_ref[...], wao_ref[...])
    merged = ma_ref[...].astype(F32) + gb_ref[...].astype(F32) * yb
    x1 = x_ref[...] + _dot(merged.astype(BF16), wo_ref[...])
    hm = _rms(x1, mng_ref[...]).astype(BF16)
    acc = x1
    for j in range(D_FF // ff_step):
        h1 = jnp.maximum(_dot(hm, w1_ref[:, j * ff_step:(j + 1) * ff_step]), 0.0)
        acc = acc + _dot((h1 * h1).astype(BF16), w2_ref[j * ff_step:(j + 1) * ff_step, :])
    out_ref[...] = _rms(acc, fng_ref[...])


def _const_spec(shape):
    nd = len(shape)
    return pl.BlockSpec(shape, lambda *_: (0,) * nd, pipeline_mode=pl.Buffered(1))


def _params(semantics):
    return pltpu.CompilerParams(dimension_semantics=semantics,
                                vmem_limit_bytes=V7X_VMEM_LIMIT_BYTES)


def _layer(x, attn_norm_g, w_in, b_gate, dw_w, dw_b, conv_ln_g, conv_ln_b, w_conv_out, q_norm_g,
           kv_norm_g, w_uq, w_uk, w_uv, w_qi, kidx_ln_g, kidx_ln_b, w_attn_out, w_o, mlp_norm_g,
           w_ff1, w_ff2, out_norm_g):
    bsz, seq_len, _ = x.shape
    n_tok = bsz * seq_len
    top_k = min(IDX_TOPK_MAX, seq_len // 4)
    tm_a = min(256, seq_len)
    tm_c = min(512, seq_len)
    assert seq_len % Q_BLOCK == 0 and seq_len % tm_a == 0 and seq_len % tm_c == 0
    assert seq_len % KEY_STEP == 0

    row = lambda v: v.reshape(1, -1).astype(F32)
    x2 = x.reshape(n_tok, D_MODEL)

    o_cq = 2 * D_MODEL
    o_ckv = o_cq + Q_LORA
    o_kidx = o_ckv + KV_LORA
    o_widx = o_kidx + IDX_DIM
    o_gate = o_widx + IDX_HEADS
    wbig = jnp.concatenate([w_in[:, 0:2 * D_MODEL], w_in[:, o_gate:o_gate + 2 * D_MODEL]],
                           axis=1).astype(BF16)
    n_small = o_gate - o_cq
    wsm = jnp.pad(w_in[:, o_cq:o_gate], ((0, 0), (0, 4 * V7X_LANES - n_small))).astype(BF16)

    dww = jnp.pad(dw_w, ((0, CONV_HALO - CONV_KERNEL), (0, 0))).astype(F32)
    klng = jnp.pad(kidx_ln_g, (0, V7X_LANES - IDX_DIM)).reshape(1, -1).astype(F32)
    klnb = jnp.pad(kidx_ln_b, (0, V7X_LANES - IDX_DIM)).reshape(1, -1).astype(F32)

    hpg = 4
    wuk_g = w_uk.reshape(N_HEADS // hpg, hpg, HEAD_DIM, KV_LORA)
    eye = jnp.eye(hpg, dtype=w_uk.dtype)
    wukbd = jnp.einsum("ghdc,hk->ghdkc", wuk_g, eye).reshape(
        N_HEADS // hpg, hpg * HEAD_DIM, hpg * KV_LORA).astype(BF16)

    tiles_per_seq = seq_len // tm_a
    tok_spec = lambda tm, n: pl.BlockSpec((tm, n), lambda t: (t, 0))
    a_in = [x2, row(attn_norm_g), wbig, wsm, row(b_gate), dww, row(dw_b), row(conv_ln_g),
            row(conv_ln_b), w_conv_out.astype(BF16), row(q_norm_g), row(kv_norm_g),
            w_uq.astype(BF16), wukbd, w_qi.astype(BF16), klng, klnb]
    a_specs = [tok_spec(tm_a, D_MODEL)] + [_const_spec(a.shape) for a in a_in[1:]]
    n_lat = N_HEADS * KV_LORA
    n_qidx = IDX_HEADS * IDX_DIM
    ma, gb, qlat, qidx, small = pl.pallas_call(
        functools.partial(_kernel_a, tm=tm_a, tiles_per_seq=tiles_per_seq),
        grid=(n_tok // tm_a,),
        in_specs=a_specs,
        out_specs=[tok_spec(tm_a, D_MODEL), tok_spec(tm_a, D_MODEL), tok_spec(tm_a, n_lat),
                   tok_spec(tm_a, n_qidx), tok_spec(tm_a, 2 * V7X_LANES)],
        out_shape=[jax.ShapeDtypeStruct((n_tok, D_MODEL), BF16),
                   jax.ShapeDtypeStruct((n_tok, D_MODEL), BF16),
                   jax.ShapeDtypeStruct((n_tok, n_lat), BF16),
                   jax.ShapeDtypeStruct((n_tok, n_qidx), BF16),
                   jax.ShapeDtypeStruct((n_tok, 2 * V7X_LANES), F32)],
        scratch_shapes=[pltpu.VMEM((CONV_HALO + tm_a, D_MODEL), F32),
                        pltpu.VMEM((tm_a, D_MODEL), BF16)],
        compiler_params=_params(("arbitrary",)),
        name="block_in",
    )(*a_in)

    ckv = small[:, 0:KV_LORA].astype(BF16).reshape(bsz, seq_len, KV_LORA)
    n_kstep = seq_len // KEY_STEP
    ckvt = jnp.swapaxes(ckv.reshape(bsz, n_kstep, KEY_STEP, KV_LORA), 2, 3)
    kx = small[:, KV_LORA:KV_LORA + IDX_DIM].astype(BF16)
    kidx = jnp.concatenate([kx, kx], axis=1).reshape(bsz, seq_len, V7X_LANES)
    widx_t = jnp.swapaxes(
        small[:, KV_LORA + IDX_DIM:KV_LORA + IDX_DIM + IDX_HEADS].reshape(bsz, seq_len, IDX_HEADS), 1, 2)
    wuvt = jnp.swapaxes(w_uv, 1, 2).astype(BF16)

    n_qb = seq_len // Q_BLOCK
    seq_spec = lambda r, c: pl.BlockSpec((None, r, c), lambda b, i: (b, 0, 0))
    o = pl.pallas_call(
        functools.partial(_kernel_b, seq_len=seq_len, top_k=top_k),
        grid=(bsz, n_qb),
        in_specs=[seq_spec(seq_len, V7X_LANES), seq_spec(seq_len, KV_LORA),
                  pl.BlockSpec((None, n_kstep, KV_LORA, KEY_STEP), lambda b, i: (b, 0, 0, 0)),
                  pl.BlockSpec((None, IDX_HEADS, Q_BLOCK), lambda b, i: (b, 0, i)),
                  pl.BlockSpec((Q_BLOCK, n_qidx), lambda b, i: (b * n_qb + i, 0)),
                  pl.BlockSpec((Q_BLOCK, n_lat), lambda b, i: (b * n_qb + i, 0)),
                  pl.BlockSpec(wuvt.shape, lambda b, i: (0, 0, 0))],
        out_specs=pl.BlockSpec((Q_BLOCK, N_HEADS * V_DIM), lambda b, i: (b * n_qb + i, 0)),
        out_shape=jax.ShapeDtypeStruct((n_tok, N_HEADS * V_DIM), BF16),
        scratch_shapes=[pltpu.VMEM((IDX_HEADS * Q_BLOCK, V7X_LANES), BF16),
                        pltpu.VMEM((N_HEADS * Q_BLOCK, KV_LORA), BF16),
                        pltpu.VMEM((seq_len, Q_BLOCK), I32),
                        pltpu.VMEM((seq_len, Q_BLOCK), F32),
                        pltpu.VMEM((N_HEADS * V_DIM, Q_BLOCK), F32)],
        compiler_params=_params(("arbitrary", "arbitrary")),
        name="sparse_attn",
    )(kidx, ckv, ckvt, widx_t, qidx, qlat, wuvt)

    c_in = [x2, o, ma, gb, w_attn_out.astype(BF16), w_o.astype(BF16), row(mlp_norm_g),
            w_ff1.astype(BF16), w_ff2.astype(BF16), row(out_norm_g)]
    c_specs = [tok_spec(tm_c, D_MODEL)] * 4 + [_const_spec(a.shape) for a in c_in[4:]]
    out = pl.pallas_call(
        functools.partial(_kernel_c, ff_step=D_MODEL),
        grid=(n_tok // tm_c,),
        in_specs=c_specs,
        out_specs=tok_spec(tm_c, D_MODEL),
        out_shape=jax.ShapeDtypeStruct((n_tok, D_MODEL), F32),
        compiler_params=_params(("arbitrary",)),
        name="block_out",
    )(*c_in)
    return out.reshape(bsz, seq_len, D_MODEL)


def kernel(x, attn_norm_g, w_in, b_gate, dw_w, dw_b, conv_ln_g, conv_ln_b, w_conv_out, q_norm_g,
           kv_norm_g, w_uq, w_uk, w_uv, w_qi, kidx_ln_g, kidx_ln_b, w_attn_out, w_o, mlp_norm_g,
           w_ff1, w_ff2, final_norm_g):
    assert attn_norm_g.shape[0] == 1, "one layer: the final RMSNorm is fused into its output kernel"
    return _layer(x, attn_norm_g[0], w_in[0], b_gate[0], dw_w[0], dw_b[0], conv_ln_g[0],
                  conv_ln_b[0], w_conv_out[0], q_norm_g[0], kv_norm_g[0], w_uq[0], w_uk[0], w_uv[0],
                  w_qi[0], kidx_ln_g[0], kidx_ln_b[0], w_attn_out[0], w_o[0], mlp_norm_g[0],
                  w_ff1[0], w_ff2[0], final_norm_g)
```

```python
import functools

import jax
import jax.numpy as jnp
from jax import lax
from jax.experimental import pallas as pl
from jax.experimental.pallas import tpu as pltpu

D_MODEL = 1024
CHUNK = 64
Q_BLOCK = 128
EPS = 1e-6
CONV_KERNEL = 31
N_HEADS = 16
HEAD_DIM = 64
V_DIM = 64
Q_LORA = 256
KV_LORA = 128
IDX_HEADS = 8
IDX_DIM = 64
IDX_TOPK_MAX = 256
D_FF = 4 * D_MODEL

V7X_LANES = 128
V7X_SUBLANES = 8
V7X_VMEM_LIMIT_BYTES = 60000 * 1024

CONV_HALO = 32
CONV_ROWS = 64
NORM_ROWS = 64
KEY_STEP = 2 * Q_BLOCK
NEG_BIG = -1e30
INT_MIN = -(2 ** 31)

F32 = jnp.float32
BF16 = jnp.bfloat16
I32 = jnp.int32


def _dot(a, b):
    return jnp.dot(a, b, preferred_element_type=F32)


def _dot_nt(a, b):
    return lax.dot_general(a, b, (((1,), (1,)), ((), ())), preferred_element_type=F32)


def _rms(x, g):
    ms = jnp.mean(x * x, axis=-1, keepdims=True)
    return x * lax.rsqrt(ms + EPS) * g


def _sigmoid(x):
    return 1.0 / (1.0 + jnp.exp(-x))


def _kernel_a(x_ref, ng_ref, wbig_ref, wsm_ref, bgate_ref, dww_ref, dwb_ref, lng_ref, lnb_ref,
              wpw_ref, qng_ref, kvng_ref, wuq_ref, wukbd_ref, wqi_ref, klng_ref, klnb_ref,
              ma_ref, gb_ref, qlat_ref, qidx_ref, small_ref,
              vbuf, ybuf, zbuf, *, tm, tiles_per_seq):
    t = pl.program_id(0)
    first = (t % tiles_per_seq) == 0

    x = x_ref[...]
    h = _rms(x, ng_ref[...]).astype(BF16)

    ua = _dot(h, wbig_ref[:, 0:D_MODEL])
    ug = _dot(h, wbig_ref[:, D_MODEL:2 * D_MODEL])
    v = ua * _sigmoid(ug)

    n_lt = D_MODEL // V7X_LANES

    @pl.when(first)
    def _():
        vbuf[:, :, 0:CONV_HALO, :] = jnp.zeros((V7X_SUBLANES, n_lt, CONV_HALO, V7X_LANES), F32)

    @pl.when(jnp.logical_not(first))
    def _():
        vbuf[:, :, 0:CONV_HALO, :] = vbuf[:, :, tm:tm + CONV_HALO, :]

    for s in range(V7X_SUBLANES):
        for lt in range(n_lt):
            vbuf[s, lt, CONV_HALO - s:CONV_HALO - s + tm, :] = v[:, lt * V7X_LANES:(lt + 1) * V7X_LANES]

    tap0 = CONV_HALO - (CONV_KERNEL - 1)

    row_groups = CONV_ROWS // V7X_SUBLANES

    def conv_chunk(r, carry):
        base = pl.multiple_of(r * CONV_ROWS, CONV_ROWS)
        for lt in range(n_lt):
            lanes = slice(lt * V7X_LANES, (lt + 1) * V7X_LANES)
            acc = jnp.zeros((row_groups, V7X_SUBLANES, V7X_LANES), F32)
            for k in range(CONV_KERNEL):
                j, s = divmod(tap0 + k, V7X_SUBLANES)
                row0 = pl.multiple_of(base + V7X_SUBLANES * j, V7X_SUBLANES)
                win = vbuf[s, lt, pl.ds(row0, CONV_ROWS), :]
                acc = acc + dww_ref[k, :, lanes][None] * win.reshape(row_groups, V7X_SUBLANES, V7X_LANES)
            ybuf[pl.ds(base, CONV_ROWS), lanes] = acc.reshape(CONV_ROWS, V7X_LANES)
        return carry

    lax.fori_loop(0, tm // CONV_ROWS, conv_chunk, 0)

    def norm_chunk(r, carry):
        base = pl.multiple_of(r * NORM_ROWS, NORM_ROWS)
        y = ybuf[pl.ds(base, NORM_ROWS), :] + dwb_ref[...]
        mu = jnp.mean(y, axis=-1, keepdims=True)
        yc = y - mu
        var = jnp.mean(yc * yc, axis=-1, keepdims=True)
        z = yc * lax.rsqrt(var + EPS) * lng_ref[...] + lnb_ref[...]
        z = z * _sigmoid(z)
        zbuf[pl.ds(base, NORM_ROWS), :] = z.astype(BF16)
        return carry

    lax.fori_loop(0, tm // NORM_ROWS, norm_chunk, 0)

    ya = _dot(zbuf[...], wpw_ref[...])
    ga = _sigmoid(_dot(h, wbig_ref[:, 2 * D_MODEL:3 * D_MODEL]) + bgate_ref[:, 0:D_MODEL])
    ma_ref[...] = (ga * ya).astype(BF16)
    gb = _sigmoid(_dot(h, wbig_ref[:, 3 * D_MODEL:4 * D_MODEL]) + bgate_ref[:, D_MODEL:2 * D_MODEL])
    gb_ref[...] = gb.astype(BF16)

    us = _dot(h, wsm_ref[...])
    cq = _rms(us[:, 0:Q_LORA], qng_ref[...]).astype(BF16)
    small_ref[:, 0:KV_LORA] = _rms(us[:, Q_LORA:Q_LORA + KV_LORA], kvng_ref[...])

    t3 = us[:, Q_LORA + KV_LORA:Q_LORA + KV_LORA + V7X_LANES]
    lane = lax.broadcasted_iota(I32, t3.shape, 1)
    is_k = lane < IDX_DIM
    mu = jnp.sum(jnp.where(is_k, t3, 0.0), axis=-1, keepdims=True) * (1.0 / IDX_DIM)
    kc = jnp.where(is_k, t3 - mu, 0.0)
    var = jnp.sum(kc * kc, axis=-1, keepdims=True) * (1.0 / IDX_DIM)
    kn = kc * lax.rsqrt(var + EPS) * klng_ref[...] + klnb_ref[...]
    wi = t3 * (IDX_HEADS ** -0.5 * IDX_DIM ** -0.5)
    small_ref[:, KV_LORA:KV_LORA + V7X_LANES] = jnp.where(
        is_k, kn, jnp.where(lane < IDX_DIM + IDX_HEADS, wi, 0.0))

    q = _dot(cq, wuq_ref[...]).astype(BF16)
    heads_per_group = wukbd_ref.shape[1] // HEAD_DIM
    lat_per_group = heads_per_group * KV_LORA
    for g in range(N_HEADS // heads_per_group):
        ql = _dot(q[:, g * heads_per_group * HEAD_DIM:(g + 1) * heads_per_group * HEAD_DIM],
                  wukbd_ref[g])
        qlat_ref[:, g * lat_per_group:(g + 1) * lat_per_group] = (ql * (HEAD_DIM ** -0.5)).astype(BF16)
    qidx_ref[...] = _dot(cq, wqi_ref[...]).astype(BF16)


def _kernel_b(kidx_ref, ckv_ref, ckvt_ref, widx_ref, qidx_ref, qlat_ref, wuvt_ref, o_ref,
              qall_scr, qlat_scr, kvb_scr, key_scr, s_scr, m_scr, l_scr, acc_scr, ot_scr,
              *, seq_len, top_k):
    i = pl.program_id(1)
    n_q = Q_BLOCK
    n_step = (i + 2) // 2

    lane = lax.broadcasted_iota(I32, (n_q, V7X_LANES), 1)
    for hh in range(IDX_HEADS):
        tile = qidx_ref[:, (hh // 2) * V7X_LANES:(hh // 2 + 1) * V7X_LANES].astype(F32)
        keep = (lane < IDX_DIM) if hh % 2 == 0 else (lane >= IDX_DIM)
        qall_scr[hh * n_q:(hh + 1) * n_q, :] = jnp.where(keep, tile, 0.0).astype(BF16)
    for hh in range(N_HEADS):
        qlat_scr[hh * n_q:(hh + 1) * n_q, 0:KV_LORA] = qlat_ref[:, hh * KV_LORA:(hh + 1) * KV_LORA]

    @pl.when(i == 0)
    def _():
        kvb_scr[:, 0:KV_LORA] = ckv_ref[...]
        eye = (lax.broadcasted_iota(I32, (n_q, n_q), 0) == lax.broadcasted_iota(I32, (n_q, n_q), 1))
        eye = jnp.where(eye, 1.0, 0.0).astype(BF16)
        for hh in range(N_HEADS):
            qlat_scr[hh * n_q:(hh + 1) * n_q, KV_LORA:KV_LORA + n_q] = eye

    qabs_chunk = (i * n_q + lax.broadcasted_iota(I32, (KEY_STEP, n_q), 1)) // CHUNK

    def score_step(c, carry):
        r0 = pl.multiple_of(c * KEY_STEP, KEY_STEP)
        lg = _dot_nt(kidx_ref[pl.ds(r0, KEY_STEP), :], qall_scr[...])
        sc = jnp.zeros((KEY_STEP, n_q), F32)
        for hh in range(IDX_HEADS):
            sc = sc + widx_ref[hh:hh + 1, :] * jnp.maximum(lg[:, hh * n_q:(hh + 1) * n_q], 0.0)
        bits = pltpu.bitcast(sc, I32)
        key = bits ^ ((bits >> 31) & 0x7FFFFFFF)
        key = jnp.where(bits == INT_MIN, 0, key)
        kabs_chunk = (r0 + lax.broadcasted_iota(I32, (KEY_STEP, n_q), 0)) // CHUNK
        key_scr[pl.ds(r0, KEY_STEP), :] = jnp.where(kabs_chunk <= qabs_chunk, key, INT_MIN)
        return carry

    lax.fori_loop(0, n_step, score_step, 0)

    def count_rows(pred):
        def body(c, cnt):
            r0 = pl.multiple_of(c * KEY_STEP, KEY_STEP)
            hit = pred(key_scr[pl.ds(r0, KEY_STEP), :], r0)
            ones = jnp.where(hit, 1, 0).astype(I32)
            return cnt + jnp.sum(ones.reshape(KEY_STEP // V7X_SUBLANES, V7X_SUBLANES, n_q), axis=0)
        cnt8 = lax.fori_loop(0, n_step, body, jnp.zeros((V7X_SUBLANES, n_q), I32))
        return jnp.sum(cnt8, axis=0, keepdims=True)

    def bisect_value(it, cur):
        cand = cur ^ jnp.left_shift(jnp.int32(1), 31 - it)
        cnt = count_rows(lambda k, r0: k >= cand)
        return jnp.where(cnt >= top_k, cand, cur)

    thr = lax.fori_loop(0, 32, bisect_value, jnp.full((1, n_q), INT_MIN, I32))
    thr = jnp.maximum(thr, INT_MIN + 1)
    n_gt = count_rows(lambda k, r0: k > thr)
    n_ge = count_rows(lambda k, r0: k >= thr)
    need = top_k - n_gt

    def tie_limit():
        n_bits = max(1, (seq_len - 1).bit_length())

        def bisect_row(it, cur):
            cand = cur + jnp.left_shift(jnp.int32(1), n_bits - 1 - it)

            def pred(k, r0):
                row = r0 + lax.broadcasted_iota(I32, (KEY_STEP, n_q), 0)
                return jnp.logical_and(k == thr, row < cand)
            cnt = count_rows(pred)
            return jnp.where(cnt < need, cand, cur)

        return lax.fori_loop(0, n_bits, bisect_row, jnp.zeros((1, n_q), I32))

    any_cut = jnp.max(n_ge.astype(F32)) > top_k
    row_lim = lax.cond(any_cut, tie_limit, lambda: jnp.full((1, n_q), seq_len, I32))

    def bias_chunk(c, carry):
        r0 = pl.multiple_of(c * KEY_STEP, KEY_STEP)
        k = key_scr[pl.ds(r0, KEY_STEP), :]
        row = r0 + lax.broadcasted_iota(I32, (KEY_STEP, n_q), 0)
        take = jnp.logical_or(k > thr, jnp.logical_and(k == thr, row <= row_lim))
        kvb_scr[pl.ds(r0, KEY_STEP), KV_LORA:KV_LORA + n_q] = jnp.where(take, 0.0, NEG_BIG).astype(BF16)
        return carry

    lax.fori_loop(0, n_step, bias_chunk, 0)

    pair = 2 * n_q
    n_pair = N_HEADS // 2
    n_col = N_HEADS * n_q
    grp = KEY_STEP // V7X_SUBLANES

    def qk_step(c, m8):
        r0 = pl.multiple_of(c * KEY_STEP, KEY_STEP)
        kvb = kvb_scr[pl.ds(r0, KEY_STEP), :]
        parts = []
        for hp in range(n_pair):
            s = _dot_nt(kvb, qlat_scr[hp * pair:(hp + 1) * pair, :])
            s_scr[pl.ds(r0, KEY_STEP), hp * pair:(hp + 1) * pair] = s
            parts.append(jnp.max(s.reshape(grp, V7X_SUBLANES, pair), axis=0))
        return jnp.maximum(m8, jnp.concatenate(parts, axis=1))

    m8 = lax.fori_loop(0, n_step, qk_step, jnp.full((V7X_SUBLANES, n_col), NEG_BIG, F32))
    m_scr[...] = jnp.max(m8, axis=0, keepdims=True)

    acc_scr[...] = jnp.zeros(acc_scr.shape, F32)

    def pv_step(c, l8):
        r0 = pl.multiple_of(c * KEY_STEP, KEY_STEP)
        kvt = ckvt_ref[c]
        parts = []
        for hp in range(n_pair):
            cols = slice(hp * pair, (hp + 1) * pair)
            p = jnp.exp(s_scr[pl.ds(r0, KEY_STEP), cols] - m_scr[:, cols])
            parts.append(jnp.sum(p.reshape(grp, V7X_SUBLANES, pair), axis=0))
            acc_scr[:, cols] += _dot(kvt, p.astype(BF16))
        return l8 + jnp.concatenate(parts, axis=1)

    l8 = lax.fori_loop(0, n_step, pv_step, jnp.zeros((V7X_SUBLANES, n_col), F32))
    l_scr[...] = jnp.sum(l8, axis=0, keepdims=True)

    for hh in range(N_HEADS):
        cols = slice(hh * n_q, (hh + 1) * n_q)
        olat = (acc_scr[:, cols] / l_scr[:, cols]).astype(BF16)
        ot_scr[hh * V_DIM:(hh + 1) * V_DIM, :] = _dot(wuvt_ref[hh], olat)
    o_ref[...] = ot_scr[...].T.astype(BF16)


def _kernel_c(x_ref, o_ref, ma_ref, gb_ref, wao_ref, wo_ref, mng_ref, w1_ref, w2_ref, fng_ref,
              out_ref, *, ff_step):
    yb = _dot(o_ref[...], wao_ref[...])
    merged = ma_ref[...].astype(F32) + gb_ref[...].astype(F32) * yb
    x1 = x_ref[...] + _dot(merged.astype(BF16), wo_ref[...])
    hm = _rms(x1, mng_ref[...]).astype(BF16)
    acc = x1
    for j in range(D_FF // ff_step):
        h1 = jnp.maximum(_dot(hm, w1_ref[:, j * ff_step:(j + 1) * ff_step]), 0.0)
        acc = acc + _dot((h1 * h1).astype(BF16), w2_ref[j * ff_step:(j + 1) * ff_step, :])
    out_ref[...] = _rms(acc, fng_ref[...])


def _const_spec(shape):
    nd = len(shape)
    return pl.BlockSpec(shape, lambda *_: (0,) * nd, pipeline_mode=pl.Buffered(1))


def _params(semantics):
    return pltpu.CompilerParams(dimension_semantics=semantics,
                                vmem_limit_bytes=V7X_VMEM_LIMIT_BYTES)


def _layer(x, attn_norm_g, w_in, b_gate, dw_w, dw_b, conv_ln_g, conv_ln_b, w_conv_out, q_norm_g,
           kv_norm_g, w_uq, w_uk, w_uv, w_qi, kidx_ln_g, kidx_ln_b, w_attn_out, w_o, mlp_norm_g,
           w_ff1, w_ff2, out_norm_g):
    bsz, seq_len, _ = x.shape
    n_tok = bsz * seq_len
    top_k = min(IDX_TOPK_MAX, seq_len // 4)
    tm_a = min(256, seq_len)
    tm_c = min(512, seq_len)
    assert seq_len % Q_BLOCK == 0 and seq_len % tm_a == 0 and seq_len % tm_c == 0
    assert seq_len % KEY_STEP == 0

    row = lambda v: v.reshape(1, -1).astype(F32)
    x2 = x.reshape(n_tok, D_MODEL)

    o_cq = 2 * D_MODEL
    o_ckv = o_cq + Q_LORA
    o_kidx = o_ckv + KV_LORA
    o_widx = o_kidx + IDX_DIM
    o_gate = o_widx + IDX_HEADS
    wbig = jnp.concatenate([w_in[:, 0:2 * D_MODEL], w_in[:, o_gate:o_gate + 2 * D_MODEL]],
                           axis=1).astype(BF16)
    n_small = o_gate - o_cq
    wsm = jnp.pad(w_in[:, o_cq:o_gate], ((0, 0), (0, 4 * V7X_LANES - n_small))).astype(BF16)

    dww = jnp.broadcast_to(dw_w.astype(F32)[:, None, :], (CONV_KERNEL, V7X_SUBLANES, D_MODEL))
    klng = jnp.pad(kidx_ln_g, (0, V7X_LANES - IDX_DIM)).reshape(1, -1).astype(F32)
    klnb = jnp.pad(kidx_ln_b, (0, V7X_LANES - IDX_DIM)).reshape(1, -1).astype(F32)

    hpg = 4
    wuk_g = w_uk.reshape(N_HEADS // hpg, hpg, HEAD_DIM, KV_LORA)
    eye = jnp.eye(hpg, dtype=w_uk.dtype)
    wukbd = jnp.einsum("ghdc,hk->ghdkc", wuk_g, eye).reshape(
        N_HEADS // hpg, hpg * HEAD_DIM, hpg * KV_LORA).astype(BF16)

    tiles_per_seq = seq_len // tm_a
    tok_spec = lambda tm, n: pl.BlockSpec((tm, n), lambda t: (t, 0))
    a_in = [x2, row(attn_norm_g), wbig, wsm, row(b_gate), dww, row(dw_b), row(conv_ln_g),
            row(conv_ln_b), w_conv_out.astype(BF16), row(q_norm_g), row(kv_norm_g),
            w_uq.astype(BF16), wukbd, w_qi.astype(BF16), klng, klnb]
    a_specs = [tok_spec(tm_a, D_MODEL)] + [_const_spec(a.shape) for a in a_in[1:]]
    n_lat = N_HEADS * KV_LORA
    n_qidx = IDX_HEADS * IDX_DIM
    ma, gb, qlat, qidx, small = pl.pallas_call(
        functools.partial(_kernel_a, tm=tm_a, tiles_per_seq=tiles_per_seq),
        grid=(n_tok // tm_a,),
        in_specs=a_specs,
        out_specs=[tok_spec(tm_a, D_MODEL), tok_spec(tm_a, D_MODEL), tok_spec(tm_a, n_lat),
                   tok_spec(tm_a, n_qidx), tok_spec(tm_a, 2 * V7X_LANES)],
        out_shape=[jax.ShapeDtypeStruct((n_tok, D_MODEL), BF16),
                   jax.ShapeDtypeStruct((n_tok, D_MODEL), BF16),
                   jax.ShapeDtypeStruct((n_tok, n_lat), BF16),
                   jax.ShapeDtypeStruct((n_tok, n_qidx), BF16),
                   jax.ShapeDtypeStruct((n_tok, 2 * V7X_LANES), F32)],
        scratch_shapes=[pltpu.VMEM((V7X_SUBLANES, D_MODEL // V7X_LANES, CONV_HALO + tm_a, V7X_LANES), F32),
                        pltpu.VMEM((tm_a, D_MODEL), F32),
                        pltpu.VMEM((tm_a, D_MODEL), BF16)],
        compiler_params=_params(("arbitrary",)),
        name="block_in",
    )(*a_in)

    ckv = small[:, 0:KV_LORA].astype(BF16).reshape(bsz, seq_len, KV_LORA)
    n_kstep = seq_len // KEY_STEP
    ckvt = jnp.swapaxes(ckv.reshape(bsz, n_kstep, KEY_STEP, KV_LORA), 2, 3)
    kx = small[:, KV_LORA:KV_LORA + IDX_DIM].astype(BF16)
    kidx = jnp.concatenate([kx, kx], axis=1).reshape(bsz, seq_len, V7X_LANES)
    widx_t = jnp.swapaxes(
        small[:, KV_LORA + IDX_DIM:KV_LORA + IDX_DIM + IDX_HEADS].reshape(bsz, seq_len, IDX_HEADS), 1, 2)
    wuvt = jnp.swapaxes(w_uv, 1, 2).astype(BF16)

    n_qb = seq_len // Q_BLOCK
    seq_spec = lambda r, c: pl.BlockSpec((None, r, c), lambda b, i: (b, 0, 0))
    o = pl.pallas_call(
        functools.partial(_kernel_b, seq_len=seq_len, top_k=top_k),
        grid=(bsz, n_qb),
        in_specs=[seq_spec(seq_len, V7X_LANES), seq_spec(seq_len, KV_LORA),
                  pl.BlockSpec((None, n_kstep, KV_LORA, KEY_STEP), lambda b, i: (b, 0, 0, 0)),
                  pl.BlockSpec((None, IDX_HEADS, Q_BLOCK), lambda b, i: (b, 0, i)),
                  pl.BlockSpec((Q_BLOCK, n_qidx), lambda b, i: (b * n_qb + i, 0)),
                  pl.BlockSpec((Q_BLOCK, n_lat), lambda b, i: (b * n_qb + i, 0)),
                  pl.BlockSpec(wuvt.shape, lambda b, i: (0, 0, 0))],
        out_specs=pl.BlockSpec((Q_BLOCK, N_HEADS * V_DIM), lambda b, i: (b * n_qb + i, 0)),
        out_shape=jax.ShapeDtypeStruct((n_tok, N_HEADS * V_DIM), BF16),
        scratch_shapes=[pltpu.VMEM((IDX_HEADS * Q_BLOCK, V7X_LANES), BF16),
                        pltpu.VMEM((N_HEADS * Q_BLOCK, KV_LORA + Q_BLOCK), BF16),
                        pltpu.VMEM((seq_len, KV_LORA + Q_BLOCK), BF16),
                        pltpu.VMEM((seq_len, Q_BLOCK), I32),
                        pltpu.VMEM((seq_len, N_HEADS * Q_BLOCK), F32),
                        pltpu.VMEM((1, N_HEADS * Q_BLOCK), F32),
                        pltpu.VMEM((1, N_HEADS * Q_BLOCK), F32),
                        pltpu.VMEM((KV_LORA, N_HEADS * Q_BLOCK), F32),
                        pltpu.VMEM((N_HEADS * V_DIM, Q_BLOCK), F32)],
        compiler_params=_params(("arbitrary", "arbitrary")),
        name="sparse_attn",
    )(kidx, ckv, ckvt, widx_t, qidx, qlat, wuvt)

    c_in = [x2, o, ma, gb, w_attn_out.astype(BF16), w_o.astype(BF16), row(mlp_norm_g),
            w_ff1.astype(BF16), w_ff2.astype(BF16), row(out_norm_g)]
    c_specs = [tok_spec(tm_c, D_MODEL)] * 4 + [_const_spec(a.shape) for a in c_in[4:]]
    out = pl.pallas_call(
        functools.partial(_kernel_c, ff_step=D_MODEL),
        grid=(n_tok // tm_c,),
        in_specs=c_specs,
        out_specs=tok_spec(tm_c, D_MODEL),
        out_shape=jax.ShapeDtypeStruct((n_tok, D_MODEL), F32),
        compiler_params=_params(("arbitrary",)),
        name="block_out",
    )(*c_in)
    return out.reshape(bsz, seq_len, D_MODEL)


def kernel(x, attn_norm_g, w_in, b_gate, dw_w, dw_b, conv_ln_g, conv_ln_b, w_conv_out, q_norm_g,
           kv_norm_g, w_uq, w_uk, w_uv, w_qi, kidx_ln_g, kidx_ln_b, w_attn_out, w_o, mlp_norm_g,
           w_ff1, w_ff2, final_norm_g):
    assert attn_norm_g.shape[0] == 1, "one layer: the final RMSNorm is fused into its output kernel"
    return _layer(x, attn_norm_g[0], w_in[0], b_gate[0], dw_w[0], dw_b[0], conv_ln_g[0],
                  conv_ln_b[0], w_conv_out[0], q_norm_g[0], kv_norm_g[0], w_uq[0], w_uk[0], w_uv[0],
                  w_qi[0], kidx_ln_g[0], kidx_ln_b[0], w_attn_out[0], w_o[0], mlp_norm_g[0],
                  w_ff1[0], w_ff2[0], final_norm_g)
```

```python
import functools

import jax
import jax.numpy as jnp
from jax import lax
from jax.experimental import pallas as pl
from jax.experimental.pallas import tpu as pltpu

D_MODEL = 1024
CHUNK = 64
Q_BLOCK = 128
EPS = 1e-6
CONV_KERNEL = 31
N_HEADS = 16
HEAD_DIM = 64
V_DIM = 64
Q_LORA = 256
KV_LORA = 128
IDX_HEADS = 8
IDX_DIM = 64
IDX_TOPK_MAX = 256
D_FF = 4 * D_MODEL

V7X_LANES = 128
V7X_SUBLANES = 8
V7X_VMEM_LIMIT_BYTES = 60000 * 1024

CONV_HALO = 32
CONV_ROWS = 64
NORM_ROWS = 64
KEY_STEP = 4 * Q_BLOCK
SCORE_ROWS = 2 * Q_BLOCK
NEG_BIG = -1e30
INT_MIN = -(2 ** 31)
LOG2_E = 1.4426950408889634
BISECT_GROUP = 4
V7X_BF16_ROWS = 16
PV_ROWS = KV_LORA + V7X_BF16_ROWS

F32 = jnp.float32
BF16 = jnp.bfloat16
I32 = jnp.int32


def _dot(a, b):
    return jnp.dot(a, b, preferred_element_type=F32)


def _dot_nt(a, b):
    return lax.dot_general(a, b, (((1,), (1,)), ((), ())), preferred_element_type=F32)


def _rms(x, g):
    ms = jnp.mean(x * x, axis=-1, keepdims=True)
    return x * lax.rsqrt(ms + EPS) * g


def _sigmoid(x):
    return 1.0 / (1.0 + jnp.exp(-x))


def _kernel_a(x_ref, ng_ref, wbig_ref, wsm_ref, bgate_ref, dww_ref, dwb_ref, lng_ref, lnb_ref,
              wpw_ref, qng_ref, kvng_ref, wuq_ref, wukbd_ref, wqi_ref, klng_ref, klnb_ref,
              ma_ref, gb_ref, qlat_ref, qidx_ref, small_ref,
              vbuf, ybuf, zbuf, gabuf, *, tm, tiles_per_seq):
    t = pl.program_id(0)
    first = (t % tiles_per_seq) == 0

    x = x_ref[...]
    h = _rms(x, ng_ref[...]).astype(BF16)

    ua = _dot(h, wbig_ref[:, 0:D_MODEL])
    ug = _dot(h, wbig_ref[:, D_MODEL:2 * D_MODEL])
    v = ua * _sigmoid(ug)

    n_lt = D_MODEL // V7X_LANES

    @pl.when(first)
    def _():
        vbuf[:, :, 0:CONV_HALO, :] = jnp.zeros((V7X_SUBLANES, n_lt, CONV_HALO, V7X_LANES), F32)

    @pl.when(jnp.logical_not(first))
    def _():
        vbuf[:, :, 0:CONV_HALO, :] = vbuf[:, :, tm:tm + CONV_HALO, :]

    for s in range(V7X_SUBLANES):
        for lt in range(n_lt):
            vbuf[s, lt, CONV_HALO - s:CONV_HALO - s + tm, :] = v[:, lt * V7X_LANES:(lt + 1) * V7X_LANES]

    tap0 = CONV_HALO - (CONV_KERNEL - 1)

    row_groups = CONV_ROWS // V7X_SUBLANES

    def conv_chunk(r, carry):
        base = r * CONV_ROWS
        for lt in range(n_lt):
            lanes = slice(lt * V7X_LANES, (lt + 1) * V7X_LANES)
            acc = jnp.zeros((row_groups, V7X_SUBLANES, V7X_LANES), F32)
            for k in range(CONV_KERNEL):
                j, s = divmod(tap0 + k, V7X_SUBLANES)
                row0 = base + V7X_SUBLANES * j
                win = vbuf[s, lt, pl.ds(row0, CONV_ROWS), :]
                acc = acc + dww_ref[k, :, lanes][None] * win.reshape(row_groups, V7X_SUBLANES, V7X_LANES)
            ybuf[pl.ds(base, CONV_ROWS), lanes] = acc.reshape(CONV_ROWS, V7X_LANES)
        return carry

    assert tm // CONV_ROWS == 4 and NORM_ROWS == CONV_ROWS
    def norm_chunk(r, carry):
        base = r * NORM_ROWS
        y = ybuf[pl.ds(base, NORM_ROWS), :] + dwb_ref[...]
        mu = jnp.mean(y, axis=-1, keepdims=True)
        yc = y - mu
        var = jnp.mean(yc * yc, axis=-1, keepdims=True)
        z = yc * lax.rsqrt(var + EPS) * lng_ref[...] + lnb_ref[...]
        z = z * _sigmoid(z)
        zbuf[pl.ds(base, NORM_ROWS), :] = z.astype(BF16)
        return carry

    conv_chunk(0, 0)
    gabuf[...] = _sigmoid(_dot(h, wbig_ref[:, 2 * D_MODEL:3 * D_MODEL]) + bgate_ref[:, 0:D_MODEL])
    conv_chunk(1, 0)
    norm_chunk(0, 0)
    gb = _sigmoid(_dot(h, wbig_ref[:, 3 * D_MODEL:4 * D_MODEL]) + bgate_ref[:, D_MODEL:2 * D_MODEL])
    gb_ref[...] = gb.astype(BF16)

    conv_chunk(2, 0)
    norm_chunk(1, 0)
    us = _dot(h, wsm_ref[...])
    cq = _rms(us[:, 0:Q_LORA], qng_ref[...]).astype(BF16)
    small_ref[:, 0:KV_LORA] = _rms(us[:, Q_LORA:Q_LORA + KV_LORA], kvng_ref[...])

    t3 = us[:, Q_LORA + KV_LORA:Q_LORA + KV_LORA + V7X_LANES]
    lane = lax.broadcasted_iota(I32, t3.shape, 1)
    is_k = lane < IDX_DIM
    mu = jnp.sum(jnp.where(is_k, t3, 0.0), axis=-1, keepdims=True) * (1.0 / IDX_DIM)
    kc = jnp.where(is_k, t3 - mu, 0.0)
    var = jnp.sum(kc * kc, axis=-1, keepdims=True) * (1.0 / IDX_DIM)
    kn = kc * lax.rsqrt(var + EPS) * klng_ref[...] + klnb_ref[...]
    wi = t3 * (IDX_HEADS ** -0.5 * IDX_DIM ** -0.5)
    small_ref[:, KV_LORA:KV_LORA + V7X_LANES] = jnp.where(
        is_k, kn, jnp.where(lane < IDX_DIM + IDX_HEADS, wi, 0.0))

    conv_chunk(3, 0)
    norm_chunk(2, 0)
    q = _dot(cq, wuq_ref[...]).astype(BF16)
    heads_per_group = wukbd_ref.shape[1] // HEAD_DIM
    lat_per_group = heads_per_group * KV_LORA
    for g in range(N_HEADS // heads_per_group):
        ql = _dot(q[:, g * heads_per_group * HEAD_DIM:(g + 1) * heads_per_group * HEAD_DIM],
                  wukbd_ref[g])
        qlat_ref[:, g * lat_per_group:(g + 1) * lat_per_group] = (ql * (HEAD_DIM ** -0.5 * LOG2_E)).astype(BF16)
    qidx_ref[...] = _dot(cq, wqi_ref[...]).astype(BF16)

    norm_chunk(3, 0)
    ya = _dot(zbuf[...], wpw_ref[...])
    ma_ref[...] = (gabuf[...] * ya).astype(BF16)


def _kernel_b(kidx_ref, ckv_ref, ckvt_ref, widx_ref, qidx_ref, qlat_ref, wuvt_ref, o_ref,
              qall_scr, qlat_scr, kvb_scr, key_scr, s_scr, m_scr, acc_scr, ot_scr,
              *, seq_len, top_k):
    i = pl.program_id(1)
    n_q = Q_BLOCK
    n_step = ((i + 1) * Q_BLOCK + KEY_STEP - 1) // KEY_STEP

    lane = lax.broadcasted_iota(I32, (n_q, V7X_LANES), 1)
    for hh in range(IDX_HEADS):
        tile = qidx_ref[:, (hh // 2) * V7X_LANES:(hh // 2 + 1) * V7X_LANES].astype(F32)
        keep = (lane < IDX_DIM) if hh % 2 == 0 else (lane >= IDX_DIM)
        qall_scr[hh * n_q:(hh + 1) * n_q, :] = jnp.where(keep, tile, 0.0).astype(BF16)
    for hh in range(N_HEADS):
        qlat_scr[hh * n_q:(hh + 1) * n_q, 0:KV_LORA] = qlat_ref[:, hh * KV_LORA:(hh + 1) * KV_LORA]

    @pl.when(i == 0)
    def _():
        kvb_scr[:, 0:KV_LORA] = ckv_ref[...]
        eye = (lax.broadcasted_iota(I32, (n_q, n_q), 0) == lax.broadcasted_iota(I32, (n_q, n_q), 1))
        eye = jnp.where(eye, 1.0, 0.0).astype(BF16)
        for hh in range(N_HEADS):
            qlat_scr[hh * n_q:(hh + 1) * n_q, KV_LORA:KV_LORA + n_q] = eye

    qabs_chunk = (i * n_q + lax.broadcasted_iota(I32, (SCORE_ROWS, n_q), 1)) // CHUNK

    def score_step(c, carry):
        for sub in range(KEY_STEP // SCORE_ROWS):
            r0 = pl.multiple_of(c * KEY_STEP + sub * SCORE_ROWS, SCORE_ROWS)
            lg = _dot_nt(kidx_ref[pl.ds(r0, SCORE_ROWS), :], qall_scr[...])
            sc = jnp.zeros((SCORE_ROWS, n_q), F32)
            for hh in range(IDX_HEADS):
                sc = sc + widx_ref[hh:hh + 1, :] * jnp.maximum(lg[:, hh * n_q:(hh + 1) * n_q], 0.0)
            bits = pltpu.bitcast(sc, I32)
            key = bits ^ ((bits >> 31) & 0x7FFFFFFF)
            key = jnp.where(bits == INT_MIN, 0, key)
            kabs_chunk = (r0 + lax.broadcasted_iota(I32, (SCORE_ROWS, n_q), 0)) // CHUNK
            key_scr[pl.ds(r0, SCORE_ROWS), :] = jnp.where(kabs_chunk <= qabs_chunk, key, INT_MIN)
        return carry

    lax.fori_loop(0, n_step, score_step, 0)

    def count_rows(pred):
        def body(c, cnt):
            r0 = pl.multiple_of(c * KEY_STEP, KEY_STEP)
            hit = pred(key_scr[pl.ds(r0, KEY_STEP), :], r0)
            ones = jnp.where(hit, 1, 0).astype(I32)
            return cnt + jnp.sum(ones.reshape(KEY_STEP // V7X_SUBLANES, V7X_SUBLANES, n_q), axis=0)
        cnt8 = lax.fori_loop(0, n_step, body, jnp.zeros((V7X_SUBLANES, n_q), I32))
        return jnp.sum(cnt8, axis=0, keepdims=True)

    def bisect_group(state):
        it, cur, settled, _ = state
        for j in range(BISECT_GROUP):
            cand = cur ^ jnp.left_shift(jnp.int32(1), 31 - (it + j))
            cnt = count_rows(lambda k, r0: k >= cand)
            step = jnp.where(cnt >= top_k, cand, cur)
            cur = jnp.where(settled > 0, cur, step)
            settled = jnp.maximum(settled, jnp.where(cnt == top_k, 1, 0))
        open_lanes = n_q - jnp.sum(settled.astype(F32))
        return it + BISECT_GROUP, cur, settled, open_lanes

    def bisect_more(state):
        it, _, _, open_lanes = state
        return jnp.logical_and(it < 32, open_lanes > 0.0)

    _, thr, _, _ = lax.while_loop(
        bisect_more, bisect_group,
        (jnp.int32(0), jnp.full((1, n_q), INT_MIN, I32), jnp.zeros((1, n_q), I32), jnp.float32(n_q)))
    thr = jnp.maximum(thr, INT_MIN + 1)
    n_gt = count_rows(lambda k, r0: k > thr)
    n_ge = count_rows(lambda k, r0: k >= thr)
    need = top_k - n_gt

    def tie_limit():
        n_bits = max(1, (seq_len - 1).bit_length())

        def bisect_row(it, cur):
            cand = cur + jnp.left_shift(jnp.int32(1), n_bits - 1 - it)

            def pred(k, r0):
                row = r0 + lax.broadcasted_iota(I32, (KEY_STEP, n_q), 0)
                return jnp.logical_and(k == thr, row < cand)
            cnt = count_rows(pred)
            return jnp.where(cnt < need, cand, cur)

        return lax.fori_loop(0, n_bits, bisect_row, jnp.zeros((1, n_q), I32))

    any_cut = jnp.max(n_ge.astype(F32)) > top_k
    row_lim = lax.cond(any_cut, tie_limit, lambda: jnp.full((1, n_q), seq_len, I32))

    def bias_chunk(c, carry):
        r0 = pl.multiple_of(c * KEY_STEP, KEY_STEP)
        k = key_scr[pl.ds(r0, KEY_STEP), :]
        row = r0 + lax.broadcasted_iota(I32, (KEY_STEP, n_q), 0)
        take = jnp.logical_or(k > thr, jnp.logical_and(k == thr, row <= row_lim))
        kvb_scr[pl.ds(r0, KEY_STEP), KV_LORA:KV_LORA + n_q] = jnp.where(take, 0.0, NEG_BIG).astype(BF16)
        return carry

    lax.fori_loop(0, n_step, bias_chunk, 0)

    pair = 2 * n_q
    n_pair = N_HEADS // 2
    n_col = N_HEADS * n_q
    grp = KEY_STEP // V7X_SUBLANES

    def qk_step(c, m8):
        r0 = pl.multiple_of(c * KEY_STEP, KEY_STEP)
        kvb = kvb_scr[pl.ds(r0, KEY_STEP), :]
        parts = []
        for hp in range(n_pair):
            s = _dot_nt(kvb, qlat_scr[hp * pair:(hp + 1) * pair, :])
            s_scr[pl.ds(r0, KEY_STEP), hp * pair:(hp + 1) * pair] = s
            parts.append(jnp.max(s.reshape(grp, V7X_SUBLANES, pair), axis=0))
        return jnp.maximum(m8, jnp.concatenate(parts, axis=1))

    m8 = lax.fori_loop(0, n_step, qk_step, jnp.full((V7X_SUBLANES, n_col), NEG_BIG, F32))
    m_scr[...] = jnp.max(m8, axis=0, keepdims=True)

    acc_scr[...] = jnp.zeros(acc_scr.shape, F32)

    def pv_step(c, carry):
        r0 = pl.multiple_of(c * KEY_STEP, KEY_STEP)
        kvt = ckvt_ref[c]
        for hp in range(n_pair):
            cols = slice(hp * pair, (hp + 1) * pair)
            p = jnp.exp2(s_scr[pl.ds(r0, KEY_STEP), cols] - m_scr[:, cols])
            acc_scr[:, cols] += _dot(kvt, p.astype(BF16))
        return carry

    lax.fori_loop(0, n_step, pv_step, 0)

    for hh in range(N_HEADS):
        cols = slice(hh * n_q, (hh + 1) * n_q)
        olat = (acc_scr[0:KV_LORA, cols] / acc_scr[KV_LORA:KV_LORA + 1, cols]).astype(BF16)
        ot_scr[hh * V_DIM:(hh + 1) * V_DIM, :] = _dot(wuvt_ref[hh], olat)
    o_ref[...] = ot_scr[...].T.astype(BF16)


def _kernel_c(x_ref, o_ref, ma_ref, gb_ref, wao_ref, wo_ref, mng_ref, w1_ref, w2_ref, fng_ref,
              out_ref, *, ff_step):
    yb = _dot(o_ref[...], wao_ref[...])
    merged = ma_ref[...].astype(F32) + gb_ref[...].astype(F32) * yb
    x1 = x_ref[...] + _dot(merged.astype(BF16), wo_ref[...])
    hm = _rms(x1, mng_ref[...]).astype(BF16)
    acc = x1
    for j in range(D_FF // ff_step):
        h1 = jnp.maximum(_dot(hm, w1_ref[:, j * ff_step:(j + 1) * ff_step]), 0.0)
        acc = acc + _dot((h1 * h1).astype(BF16), w2_ref[j * ff_step:(j + 1) * ff_step, :])
    out_ref[...] = _rms(acc, fng_ref[...])


def _const_spec(shape):
    nd = len(shape)
    return pl.BlockSpec(shape, lambda *_: (0,) * nd, pipeline_mode=pl.Buffered(1))


def _params(semantics):
    return pltpu.CompilerParams(dimension_semantics=semantics,
                                vmem_limit_bytes=V7X_VMEM_LIMIT_BYTES)


def _layer(x, attn_norm_g, w_in, b_gate, dw_w, dw_b, conv_ln_g, conv_ln_b, w_conv_out, q_norm_g,
           kv_norm_g, w_uq, w_uk, w_uv, w_qi, kidx_ln_g, kidx_ln_b, w_attn_out, w_o, mlp_norm_g,
           w_ff1, w_ff2, out_norm_g):
    bsz, seq_len, _ = x.shape
    n_tok = bsz * seq_len
    top_k = min(IDX_TOPK_MAX, seq_len // 4)
    tm_a = min(256, seq_len)
    tm_c = min(512, seq_len)
    assert seq_len % Q_BLOCK == 0 and seq_len % tm_a == 0 and seq_len % tm_c == 0
    assert seq_len % KEY_STEP == 0

    row = lambda v: v.reshape(1, -1).astype(F32)
    x2 = x.reshape(n_tok, D_MODEL)

    o_cq = 2 * D_MODEL
    o_ckv = o_cq + Q_LORA
    o_kidx = o_ckv + KV_LORA
    o_widx = o_kidx + IDX_DIM
    o_gate = o_widx + IDX_HEADS
    wbig = jnp.concatenate([w_in[:, 0:2 * D_MODEL], w_in[:, o_gate:o_gate + 2 * D_MODEL]],
                           axis=1).astype(BF16)
    n_small = o_gate - o_cq
    wsm = jnp.pad(w_in[:, o_cq:o_gate], ((0, 0), (0, 4 * V7X_LANES - n_small))).astype(BF16)

    dww = jnp.broadcast_to(dw_w.astype(F32)[:, None, :], (CONV_KERNEL, V7X_SUBLANES, D_MODEL))
    klng = jnp.pad(kidx_ln_g, (0, V7X_LANES - IDX_DIM)).reshape(1, -1).astype(F32)
    klnb = jnp.pad(kidx_ln_b, (0, V7X_LANES - IDX_DIM)).reshape(1, -1).astype(F32)

    hpg = 4
    wuk_g = w_uk.reshape(N_HEADS // hpg, hpg, HEAD_DIM, KV_LORA)
    eye = jnp.eye(hpg, dtype=w_uk.dtype)
    wukbd = jnp.einsum("ghdc,hk->ghdkc", wuk_g, eye).reshape(
        N_HEADS // hpg, hpg * HEAD_DIM, hpg * KV_LORA).astype(BF16)

    tiles_per_seq = seq_len // tm_a
    tok_spec = lambda tm, n: pl.BlockSpec((tm, n), lambda t: (t, 0))
    a_in = [x2, row(attn_norm_g), wbig, wsm, row(b_gate), dww, row(dw_b), row(conv_ln_g),
            row(conv_ln_b), w_conv_out.astype(BF16), row(q_norm_g), row(kv_norm_g),
            w_uq.astype(BF16), wukbd, w_qi.astype(BF16), klng, klnb]
    a_specs = [tok_spec(tm_a, D_MODEL)] + [_const_spec(a.shape) for a in a_in[1:]]
    n_lat = N_HEADS * KV_LORA
    n_qidx = IDX_HEADS * IDX_DIM
    ma, gb, qlat, qidx, small = pl.pallas_call(
        functools.partial(_kernel_a, tm=tm_a, tiles_per_seq=tiles_per_seq),
        grid=(n_tok // tm_a,),
        in_specs=a_specs,
        out_specs=[tok_spec(tm_a, D_MODEL), tok_spec(tm_a, D_MODEL), tok_spec(tm_a, n_lat),
                   tok_spec(tm_a, n_qidx), tok_spec(tm_a, 2 * V7X_LANES)],
        out_shape=[jax.ShapeDtypeStruct((n_tok, D_MODEL), BF16),
                   jax.ShapeDtypeStruct((n_tok, D_MODEL), BF16),
                   jax.ShapeDtypeStruct((n_tok, n_lat), BF16),
                   jax.ShapeDtypeStruct((n_tok, n_qidx), BF16),
                   jax.ShapeDtypeStruct((n_tok, 2 * V7X_LANES), F32)],
        scratch_shapes=[pltpu.VMEM((V7X_SUBLANES, D_MODEL // V7X_LANES, CONV_HALO + tm_a, V7X_LANES), F32),
                        pltpu.VMEM((tm_a, D_MODEL), F32),
                        pltpu.VMEM((tm_a, D_MODEL), BF16),
                        pltpu.VMEM((tm_a, D_MODEL), F32)],
        compiler_params=_params(("arbitrary",)),
        name="block_in",
    )(*a_in)

    ckv = small[:, 0:KV_LORA].astype(BF16).reshape(bsz, seq_len, KV_LORA)
    n_kstep = seq_len // KEY_STEP
    ckvt = jnp.swapaxes(ckv.reshape(bsz, n_kstep, KEY_STEP, KV_LORA), 2, 3)
    ones_row = (lax.broadcasted_iota(I32, (bsz, n_kstep, V7X_BF16_ROWS, KEY_STEP), 2) == 0).astype(BF16)
    ckvt = jnp.concatenate([ckvt, ones_row], axis=2)
    kx = small[:, KV_LORA:KV_LORA + IDX_DIM].astype(BF16)
    kidx = jnp.concatenate([kx, kx], axis=1).reshape(bsz, seq_len, V7X_LANES)
    widx_t = jnp.swapaxes(
        small[:, KV_LORA + IDX_DIM:KV_LORA + IDX_DIM + IDX_HEADS].reshape(bsz, seq_len, IDX_HEADS), 1, 2)
    wuvt = jnp.swapaxes(w_uv, 1, 2).astype(BF16)

    n_qb = seq_len // Q_BLOCK
    seq_spec = lambda r, c: pl.BlockSpec((None, r, c), lambda b, i: (b, 0, 0))
    o = pl.pallas_call(
        functools.partial(_kernel_b, seq_len=seq_len, top_k=top_k),
        grid=(bsz, n_qb),
        in_specs=[seq_spec(seq_len, V7X_LANES), seq_spec(seq_len, KV_LORA),
                  pl.BlockSpec((None, n_kstep, PV_ROWS, KEY_STEP), lambda b, i: (b, 0, 0, 0)),
                  pl.BlockSpec((None, IDX_HEADS, Q_BLOCK), lambda b, i: (b, 0, i)),
                  pl.BlockSpec((Q_BLOCK, n_qidx), lambda b, i: (b * n_qb + i, 0)),
                  pl.BlockSpec((Q_BLOCK, n_lat), lambda b, i: (b * n_qb + i, 0)),
                  pl.BlockSpec(wuvt.shape, lambda b, i: (0, 0, 0))],
        out_specs=pl.BlockSpec((Q_BLOCK, N_HEADS * V_DIM), lambda b, i: (b * n_qb + i, 0)),
        out_shape=jax.ShapeDtypeStruct((n_tok, N_HEADS * V_DIM), BF16),
        scratch_shapes=[pltpu.VMEM((IDX_HEADS * Q_BLOCK, V7X_LANES), BF16),
                        pltpu.VMEM((N_HEADS * Q_BLOCK, KV_LORA + Q_BLOCK), BF16),
                        pltpu.VMEM((seq_len, KV_LORA + Q_BLOCK), BF16),
                        pltpu.VMEM((seq_len, Q_BLOCK), I32),
                        pltpu.VMEM((seq_len, N_HEADS * Q_BLOCK), F32),
                        pltpu.VMEM((1, N_HEADS * Q_BLOCK), F32),
                        pltpu.VMEM((PV_ROWS, N_HEADS * Q_BLOCK), F32),
                        pltpu.VMEM((N_HEADS * V_DIM, Q_BLOCK), F32)],
        compiler_params=_params(("arbitrary", "arbitrary")),
        name="sparse_attn",
    )(kidx, ckv, ckvt, widx_t, qidx, qlat, wuvt)

    c_in = [x2, o, ma, gb, w_attn_out.astype(BF16), w_o.astype(BF16), row(mlp_norm_g),
            w_ff1.astype(BF16), w_ff2.astype(BF16), row(out_norm_g)]
    c_specs = [tok_spec(tm_c, D_MODEL)] * 4 + [_const_spec(a.shape) for a in c_in[4:]]
    out = pl.pallas_call(
        functools.partial(_kernel_c, ff_step=D_MODEL),
        grid=(n_tok // tm_c,),
        in_specs=c_specs,
        out_specs=tok_spec(tm_c, D_MODEL),
        out_shape=jax.ShapeDtypeStruct((n_tok, D_MODEL), F32),
        compiler_params=_params(("arbitrary",)),
        name="block_out",
    )(*c_in)
    return out.reshape(bsz, seq_len, D_MODEL)


def kernel(x, attn_norm_g, w_in, b_gate, dw_w, dw_b, conv_ln_g, conv_ln_b, w_conv_out, q_norm_g,
           kv_norm_g, w_uq, w_uk, w_uv, w_qi, kidx_ln_g, kidx_ln_b, w_attn_out, w_o, mlp_norm_g,
           w_ff1, w_ff2, final_norm_g):
    assert attn_norm_g.shape[0] == 1, "one layer: the final RMSNorm is fused into its output kernel"
    return _layer(x, attn_norm_g[0], w_in[0], b_gate[0], dw_w[0], dw_b[0], conv_ln_g[0],
                  conv_ln_b[0], w_conv_out[0], q_norm_g[0], kv_norm_g[0], w_uq[0], w_uk[0], w_uv[0],
                  w_qi[0], kidx_ln_g[0], kidx_ln_b[0], w_attn_out[0], w_o[0], mlp_norm_g[0],
                  w_ff1[0], w_ff2[0], final_norm_g)
```

```python
import functools

import jax
import jax.numpy as jnp
from jax import lax
from jax.experimental import pallas as pl
from jax.experimental.pallas import tpu as pltpu

D_MODEL = 1024
CHUNK = 64
Q_BLOCK = 128
EPS = 1e-6
CONV_KERNEL = 31
N_HEADS = 16
HEAD_DIM = 64
V_DIM = 64
Q_LORA = 256
KV_LORA = 128
IDX_HEADS = 8
IDX_DIM = 64
IDX_TOPK_MAX = 256
D_FF = 4 * D_MODEL

V7X_LANES = 128
V7X_SUBLANES = 8
V7X_VMEM_LIMIT_BYTES = 60000 * 1024

CONV_HALO = 32
CONV_ROWS = 64
NORM_ROWS = 64
KEY_STEP = 4 * Q_BLOCK
HALF_STEP = KEY_STEP // 2
SCORE_ROWS = 2 * Q_BLOCK
NEG_BIG = -1e30
INT_MIN = -(2 ** 31)
LOG2_E = 1.4426950408889634
BISECT_GROUP = 4
V7X_BF16_ROWS = 16
PV_ROWS = KV_LORA + V7X_BF16_ROWS

F32 = jnp.float32
BF16 = jnp.bfloat16
I32 = jnp.int32


def _dot(a, b):
    return jnp.dot(a, b, preferred_element_type=F32)


def _dot_nt(a, b):
    return lax.dot_general(a, b, (((1,), (1,)), ((), ())), preferred_element_type=F32)


def _rms(x, g):
    ms = jnp.mean(x * x, axis=-1, keepdims=True)
    return x * lax.rsqrt(ms + EPS) * g


def _sigmoid(x):
    return 1.0 / (1.0 + jnp.exp(-x))


def _kernel_a(x_ref, ng_ref, wbig_ref, wsm_ref, bgate_ref, dww_ref, dwb_ref, lng_ref, lnb_ref,
              wpw_ref, qng_ref, kvng_ref, wuq_ref, wukbd_ref, wqi_ref, klng_ref, klnb_ref,
              ma_ref, gb_ref, qlat_ref, qidx_ref, ckv_ref, ckvt_ref, kidx_ref, widx_ref,
              vbuf, ybuf, zbuf, gabuf, *, tm, tiles_per_seq):
    t = pl.program_id(0)
    first = (t % tiles_per_seq) == 0

    x = x_ref[...]
    h = _rms(x, ng_ref[...]).astype(BF16)

    ua = _dot(h, wbig_ref[:, 0:D_MODEL])
    ug = _dot(h, wbig_ref[:, D_MODEL:2 * D_MODEL])
    v = ua * _sigmoid(ug)

    n_lt = D_MODEL // V7X_LANES

    @pl.when(first)
    def _():
        vbuf[:, :, 0:CONV_HALO, :] = jnp.zeros((V7X_SUBLANES, n_lt, CONV_HALO, V7X_LANES), F32)

    @pl.when(jnp.logical_not(first))
    def _():
        vbuf[:, :, 0:CONV_HALO, :] = vbuf[:, :, tm:tm + CONV_HALO, :]

    for s in range(V7X_SUBLANES):
        for lt in range(n_lt):
            vbuf[s, lt, CONV_HALO - s:CONV_HALO - s + tm, :] = v[:, lt * V7X_LANES:(lt + 1) * V7X_LANES]

    tap0 = CONV_HALO - (CONV_KERNEL - 1)

    row_groups = CONV_ROWS // V7X_SUBLANES

    def conv_chunk(r, carry):
        base = r * CONV_ROWS
        for lt in range(n_lt):
            lanes = slice(lt * V7X_LANES, (lt + 1) * V7X_LANES)
            acc = jnp.zeros((row_groups, V7X_SUBLANES, V7X_LANES), F32)
            for k in range(CONV_KERNEL):
                j, s = divmod(tap0 + k, V7X_SUBLANES)
                row0 = base + V7X_SUBLANES * j
                win = vbuf[s, lt, pl.ds(row0, CONV_ROWS), :]
                acc = acc + dww_ref[k, :, lanes][None] * win.reshape(row_groups, V7X_SUBLANES, V7X_LANES)
            ybuf[pl.ds(base, CONV_ROWS), lanes] = acc.reshape(CONV_ROWS, V7X_LANES)
        return carry

    assert tm // CONV_ROWS == 4 and NORM_ROWS == CONV_ROWS
    def norm_chunk(r, carry):
        base = r * NORM_ROWS
        y = ybuf[pl.ds(base, NORM_ROWS), :] + dwb_ref[...]
        mu = jnp.mean(y, axis=-1, keepdims=True)
        yc = y - mu
        var = jnp.mean(yc * yc, axis=-1, keepdims=True)
        z = yc * lax.rsqrt(var + EPS) * lng_ref[...] + lnb_ref[...]
        z = z * _sigmoid(z)
        zbuf[pl.ds(base, NORM_ROWS), :] = z.astype(BF16)
        return carry

    conv_chunk(0, 0)
    gabuf[...] = _sigmoid(_dot(h, wbig_ref[:, 2 * D_MODEL:3 * D_MODEL]) + bgate_ref[:, 0:D_MODEL])
    conv_chunk(1, 0)
    norm_chunk(0, 0)
    gb = _sigmoid(_dot(h, wbig_ref[:, 3 * D_MODEL:4 * D_MODEL]) + bgate_ref[:, D_MODEL:2 * D_MODEL])
    gb_ref[...] = gb.astype(BF16)

    conv_chunk(2, 0)
    norm_chunk(1, 0)
    us = _dot(h, wsm_ref[...])
    cq = _rms(us[:, 0:Q_LORA], qng_ref[...]).astype(BF16)
    ckv = _rms(us[:, Q_LORA:Q_LORA + KV_LORA], kvng_ref[...])
    ckv_ref[...] = ckv.astype(BF16)
    ckvt_ref[0:KV_LORA, :] = ckv.T.astype(BF16)
    ckvt_ref[KV_LORA:PV_ROWS, :] = jnp.where(
        lax.broadcasted_iota(I32, (PV_ROWS - KV_LORA, tm), 0) == 0, 1.0, 0.0).astype(BF16)

    t3 = us[:, Q_LORA + KV_LORA:Q_LORA + KV_LORA + V7X_LANES]
    lane = lax.broadcasted_iota(I32, t3.shape, 1)
    is_k = lane < IDX_DIM
    mu = jnp.sum(jnp.where(is_k, t3, 0.0), axis=-1, keepdims=True) * (1.0 / IDX_DIM)
    kc = jnp.where(is_k, t3 - mu, 0.0)
    var = jnp.sum(kc * kc, axis=-1, keepdims=True) * (1.0 / IDX_DIM)
    kn = kc * lax.rsqrt(var + EPS) * klng_ref[...] + klnb_ref[...]
    kidx_ref[...] = (kn + pltpu.roll(kn, IDX_DIM, 1)).astype(BF16)
    wi = t3 * (IDX_HEADS ** -0.5 * IDX_DIM ** -0.5)
    widx_ref[...] = wi.T[IDX_DIM:IDX_DIM + IDX_HEADS, :]

    conv_chunk(3, 0)
    norm_chunk(2, 0)
    q = _dot(cq, wuq_ref[...]).astype(BF16)
    heads_per_group = wukbd_ref.shape[1] // HEAD_DIM
    lat_per_group = heads_per_group * KV_LORA
    for g in range(N_HEADS // heads_per_group):
        ql = _dot(q[:, g * heads_per_group * HEAD_DIM:(g + 1) * heads_per_group * HEAD_DIM],
                  wukbd_ref[g])
        qlat_ref[:, g * lat_per_group:(g + 1) * lat_per_group] = (ql * (HEAD_DIM ** -0.5 * LOG2_E)).astype(BF16)
    qidx_ref[...] = _dot(cq, wqi_ref[...]).astype(BF16)

    norm_chunk(3, 0)
    ya = _dot(zbuf[...], wpw_ref[...])
    ma_ref[...] = (gabuf[...] * ya).astype(BF16)


def _kernel_b(kidx_ref, ckv_ref, ckvt_ref, widx_ref, qidx_ref, qlat_ref, wuvt_ref, o_ref,
              qall_scr, qlat_scr, kvb_scr, key_scr, s_scr, m_scr, acc_scr, ot_scr,
              *, seq_len, top_k):
    i = pl.program_id(1)
    n_q = Q_BLOCK
    blk_per_step = KEY_STEP // Q_BLOCK
    rem = (i + 1) % blk_per_step
    has_half = jnp.logical_and(rem > 0, rem <= HALF_STEP // Q_BLOCK)
    n_full = (i + 1) // blk_per_step + jnp.where(rem > HALF_STEP // Q_BLOCK, 1, 0)
    tail0 = pl.multiple_of(n_full * KEY_STEP, KEY_STEP)

    def over_keys(step_fn, init):
        def full(c, carry):
            return step_fn(pl.multiple_of(c * KEY_STEP, KEY_STEP), KEY_STEP, c, carry)
        carry = lax.fori_loop(0, n_full, full, init)
        return lax.cond(has_half, lambda cr: step_fn(tail0, HALF_STEP, n_full, cr), lambda cr: cr, carry)

    lane = lax.broadcasted_iota(I32, (n_q, V7X_LANES), 1)
    for hh in range(IDX_HEADS):
        tile = qidx_ref[:, (hh // 2) * V7X_LANES:(hh // 2 + 1) * V7X_LANES].astype(F32)
        keep = (lane < IDX_DIM) if hh % 2 == 0 else (lane >= IDX_DIM)
        qall_scr[hh * n_q:(hh + 1) * n_q, :] = jnp.where(keep, tile, 0.0).astype(BF16)
    for hh in range(N_HEADS):
        qlat_scr[hh * n_q:(hh + 1) * n_q, 0:KV_LORA] = qlat_ref[:, hh * KV_LORA:(hh + 1) * KV_LORA]

    @pl.when(i == 0)
    def _():
        kvb_scr[:, 0:KV_LORA] = ckv_ref[...]
        eye = (lax.broadcasted_iota(I32, (n_q, n_q), 0) == lax.broadcasted_iota(I32, (n_q, n_q), 1))
        eye = jnp.where(eye, 1.0, 0.0).astype(BF16)
        for hh in range(N_HEADS):
            qlat_scr[hh * n_q:(hh + 1) * n_q, KV_LORA:KV_LORA + n_q] = eye

    qabs_chunk = (i * n_q + lax.broadcasted_iota(I32, (SCORE_ROWS, n_q), 1)) // CHUNK

    def score_step(first, rows, c, carry):
        for sub in range(rows // SCORE_ROWS):
            r0 = pl.multiple_of(first + sub * SCORE_ROWS, SCORE_ROWS)
            lg = _dot_nt(kidx_ref[pl.ds(r0, SCORE_ROWS), :], qall_scr[...])
            sc = jnp.zeros((SCORE_ROWS, n_q), F32)
            for hh in range(IDX_HEADS):
                sc = sc + widx_ref[hh:hh + 1, :] * jnp.maximum(lg[:, hh * n_q:(hh + 1) * n_q], 0.0)
            bits = pltpu.bitcast(sc, I32)
            key = bits ^ ((bits >> 31) & 0x7FFFFFFF)
            key = jnp.where(bits == INT_MIN, 0, key)
            kabs_chunk = (r0 + lax.broadcasted_iota(I32, (SCORE_ROWS, n_q), 0)) // CHUNK
            key_scr[pl.ds(r0, SCORE_ROWS), :] = jnp.where(kabs_chunk <= qabs_chunk, key, INT_MIN)
        return carry

    over_keys(score_step, 0)

    def count_rows(pred):
        def body(r0, rows, c, cnt):
            hit = pred(key_scr[pl.ds(r0, rows), :], r0)
            ones = jnp.where(hit, 1, 0).astype(I32)
            return cnt + jnp.sum(ones.reshape(rows // V7X_SUBLANES, V7X_SUBLANES, n_q), axis=0)
        cnt8 = over_keys(body, jnp.zeros((V7X_SUBLANES, n_q), I32))
        return jnp.sum(cnt8, axis=0, keepdims=True)

    def bisect_group(state):
        it, cur, settled, _ = state
        for j in range(BISECT_GROUP):
            cand = cur ^ jnp.left_shift(jnp.int32(1), 31 - (it + j))
            cnt = count_rows(lambda k, r0: k >= cand)
            step = jnp.where(cnt >= top_k, cand, cur)
            cur = jnp.where(settled > 0, cur, step)
            settled = jnp.maximum(settled, jnp.where(cnt == top_k, 1, 0))
        open_lanes = n_q - jnp.sum(settled.astype(F32))
        return it + BISECT_GROUP, cur, settled, open_lanes

    def bisect_more(state):
        it, _, _, open_lanes = state
        return jnp.logical_and(it < 32, open_lanes > 0.0)

    _, thr, _, _ = lax.while_loop(
        bisect_more, bisect_group,
        (jnp.int32(0), jnp.full((1, n_q), INT_MIN, I32), jnp.zeros((1, n_q), I32), jnp.float32(n_q)))
    thr = jnp.maximum(thr, INT_MIN + 1)
    n_gt = count_rows(lambda k, r0: k > thr)
    n_ge = count_rows(lambda k, r0: k >= thr)
    need = top_k - n_gt

    def tie_limit():
        n_bits = max(1, (seq_len - 1).bit_length())

        def bisect_row(it, cur):
            cand = cur + jnp.left_shift(jnp.int32(1), n_bits - 1 - it)

            def pred(k, r0):
                row = r0 + lax.broadcasted_iota(I32, k.shape, 0)
                return jnp.logical_and(k == thr, row < cand)
            cnt = count_rows(pred)
            return jnp.where(cnt < need, cand, cur)

        return lax.fori_loop(0, n_bits, bisect_row, jnp.zeros((1, n_q), I32))

    any_cut = jnp.max(n_ge.astype(F32)) > top_k
    row_lim = lax.cond(any_cut, tie_limit, lambda: jnp.full((1, n_q), seq_len, I32))

    def bias_chunk(r0, rows, c, carry):
        k = key_scr[pl.ds(r0, rows), :]
        row = r0 + lax.broadcasted_iota(I32, (rows, n_q), 0)
        take = jnp.logical_or(k > thr, jnp.logical_and(k == thr, row <= row_lim))
        kvb_scr[pl.ds(r0, rows), KV_LORA:KV_LORA + n_q] = jnp.where(take, 0.0, NEG_BIG).astype(BF16)
        return carry

    over_keys(bias_chunk, 0)

    pair = 2 * n_q
    n_pair = N_HEADS // 2
    n_col = N_HEADS * n_q

    def qk_step(r0, rows, c, m8):
        kvb = kvb_scr[pl.ds(r0, rows), :]
        parts = []
        for hp in range(n_pair):
            s = _dot_nt(kvb, qlat_scr[hp * pair:(hp + 1) * pair, :])
            s_scr[pl.ds(r0, rows), hp * pair:(hp + 1) * pair] = s
            parts.append(jnp.max(s.reshape(rows // V7X_SUBLANES, V7X_SUBLANES, pair), axis=0))
        return jnp.maximum(m8, jnp.concatenate(parts, axis=1))

    m8 = over_keys(qk_step, jnp.full((V7X_SUBLANES, n_col), NEG_BIG, F32))
    m_scr[...] = jnp.max(m8, axis=0, keepdims=True)

    acc_scr[...] = jnp.zeros(acc_scr.shape, F32)

    def pv_step(r0, rows, c, carry):
        kvt = ckvt_ref[c, :, 0:rows]
        for hp in range(n_pair):
            cols = slice(hp * pair, (hp + 1) * pair)
            p = jnp.exp2(s_scr[pl.ds(r0, rows), cols] - m_scr[:, cols])
            acc_scr[:, cols] += _dot(kvt, p.astype(BF16))
        return carry

    over_keys(pv_step, 0)

    for hh in range(N_HEADS):
        cols = slice(hh * n_q, (hh + 1) * n_q)
        inv_l = 1.0 / acc_scr[KV_LORA:KV_LORA + 1, cols]
        olat = (acc_scr[0:KV_LORA, cols] * inv_l).astype(BF16)
        ot_scr[hh * V_DIM:(hh + 1) * V_DIM, :] = _dot(wuvt_ref[hh], olat)
    o_ref[...] = ot_scr[...].T.astype(BF16)


def _kernel_c(x_ref, o_ref, ma_ref, gb_ref, wao_ref, wo_ref, mng_ref, w1_ref, w2_ref, fng_ref,
              out_ref, *, ff_step):
    yb = _dot(o_ref[...], wao_ref[...])
    merged = ma_ref[...].astype(F32) + gb_ref[...].astype(F32) * yb
    x1 = x_ref[...] + _dot(merged.astype(BF16), wo_ref[...])
    hm = _rms(x1, mng_ref[...]).astype(BF16)
    acc = x1
    for j in range(D_FF // ff_step):
        h1 = jnp.maximum(_dot(hm, w1_ref[:, j * ff_step:(j + 1) * ff_step]), 0.0)
        acc = acc + _dot((h1 * h1).astype(BF16), w2_ref[j * ff_step:(j + 1) * ff_step, :])
    out_ref[...] = _rms(acc, fng_ref[...])


def _const_spec(shape):
    nd = len(shape)
    return pl.BlockSpec(shape, lambda *_: (0,) * nd, pipeline_mode=pl.Buffered(1))


def _params(semantics):
    return pltpu.CompilerParams(dimension_semantics=semantics,
                                vmem_limit_bytes=V7X_VMEM_LIMIT_BYTES)


def _layer(x, attn_norm_g, w_in, b_gate, dw_w, dw_b, conv_ln_g, conv_ln_b, w_conv_out, q_norm_g,
           kv_norm_g, w_uq, w_uk, w_uv, w_qi, kidx_ln_g, kidx_ln_b, w_attn_out, w_o, mlp_norm_g,
           w_ff1, w_ff2, out_norm_g):
    bsz, seq_len, _ = x.shape
    n_tok = bsz * seq_len
    top_k = min(IDX_TOPK_MAX, seq_len // 4)
    tm_a = min(256, seq_len)
    tm_c = min(512, seq_len)
    assert seq_len % Q_BLOCK == 0 and seq_len % tm_a == 0 and seq_len % tm_c == 0
    assert seq_len % KEY_STEP == 0

    row = lambda v: v.reshape(1, -1).astype(F32)
    x2 = x.reshape(n_tok, D_MODEL)

    o_cq = 2 * D_MODEL
    o_ckv = o_cq + Q_LORA
    o_kidx = o_ckv + KV_LORA
    o_widx = o_kidx + IDX_DIM
    o_gate = o_widx + IDX_HEADS
    wbig = jnp.concatenate([w_in[:, 0:2 * D_MODEL], w_in[:, o_gate:o_gate + 2 * D_MODEL]],
                           axis=1).astype(BF16)
    n_small = o_gate - o_cq
    wsm = jnp.pad(w_in[:, o_cq:o_gate], ((0, 0), (0, 4 * V7X_LANES - n_small))).astype(BF16)

    dww = jnp.broadcast_to(dw_w.astype(F32)[:, None, :], (CONV_KERNEL, V7X_SUBLANES, D_MODEL))
    klng = jnp.pad(kidx_ln_g, (0, V7X_LANES - IDX_DIM)).reshape(1, -1).astype(F32)
    klnb = jnp.pad(kidx_ln_b, (0, V7X_LANES - IDX_DIM)).reshape(1, -1).astype(F32)

    hpg = 4
    wuk_g = w_uk.reshape(N_HEADS // hpg, hpg, HEAD_DIM, KV_LORA)
    eye = jnp.eye(hpg, dtype=w_uk.dtype)
    wukbd = jnp.einsum("ghdc,hk->ghdkc", wuk_g, eye).reshape(
        N_HEADS // hpg, hpg * HEAD_DIM, hpg * KV_LORA).astype(BF16)

    tiles_per_seq = seq_len // tm_a
    tok_spec = lambda tm, n: pl.BlockSpec((tm, n), lambda t: (t, 0))
    a_in = [x2, row(attn_norm_g), wbig, wsm, row(b_gate), dww, row(dw_b), row(conv_ln_g),
            row(conv_ln_b), w_conv_out.astype(BF16), row(q_norm_g), row(kv_norm_g),
            w_uq.astype(BF16), wukbd, w_qi.astype(BF16), klng, klnb]
    a_specs = [tok_spec(tm_a, D_MODEL)] + [_const_spec(a.shape) for a in a_in[1:]]
    n_lat = N_HEADS * KV_LORA
    n_qidx = IDX_HEADS * IDX_DIM
    n_kstep = seq_len // KEY_STEP
    tiles_per_kstep = KEY_STEP // tm_a
    ma, gb, qlat, qidx, ckv, ckvt, kidx, widx_t = pl.pallas_call(
        functools.partial(_kernel_a, tm=tm_a, tiles_per_seq=tiles_per_seq),
        grid=(n_tok // tm_a,),
        in_specs=a_specs,
        out_specs=[tok_spec(tm_a, D_MODEL), tok_spec(tm_a, D_MODEL), tok_spec(tm_a, n_lat),
                   tok_spec(tm_a, n_qidx), tok_spec(tm_a, KV_LORA),
                   pl.BlockSpec((None, None, PV_ROWS, tm_a),
                                lambda t: (t // tiles_per_seq, (t % tiles_per_seq) // tiles_per_kstep, 0,
                                           t % tiles_per_kstep)),
                   tok_spec(tm_a, V7X_LANES),
                   pl.BlockSpec((None, IDX_HEADS, tm_a), lambda t: (t // tiles_per_seq, 0, t % tiles_per_seq))],
        out_shape=[jax.ShapeDtypeStruct((n_tok, D_MODEL), BF16),
                   jax.ShapeDtypeStruct((n_tok, D_MODEL), BF16),
                   jax.ShapeDtypeStruct((n_tok, n_lat), BF16),
                   jax.ShapeDtypeStruct((n_tok, n_qidx), BF16),
                   jax.ShapeDtypeStruct((n_tok, KV_LORA), BF16),
                   jax.ShapeDtypeStruct((bsz, n_kstep, PV_ROWS, KEY_STEP), BF16),
                   jax.ShapeDtypeStruct((n_tok, V7X_LANES), BF16),
                   jax.ShapeDtypeStruct((bsz, IDX_HEADS, seq_len), F32)],
        scratch_shapes=[pltpu.VMEM((V7X_SUBLANES, D_MODEL // V7X_LANES, CONV_HALO + tm_a, V7X_LANES), F32),
                        pltpu.VMEM((tm_a, D_MODEL), F32),
                        pltpu.VMEM((tm_a, D_MODEL), BF16),
                        pltpu.VMEM((tm_a, D_MODEL), F32)],
        compiler_params=_params(("arbitrary",)),
        name="block_in",
    )(*a_in)

    ckv = ckv.reshape(bsz, seq_len, KV_LORA)
    kidx = kidx.reshape(bsz, seq_len, V7X_LANES)
    wuvt = jnp.swapaxes(w_uv, 1, 2).astype(BF16)

    n_qb = seq_len // Q_BLOCK
    seq_spec = lambda r, c: pl.BlockSpec((None, r, c), lambda b, i: (b, 0, 0))
    o = pl.pallas_call(
        functools.partial(_kernel_b, seq_len=seq_len, top_k=top_k),
        grid=(bsz, n_qb),
        in_specs=[seq_spec(seq_len, V7X_LANES), seq_spec(seq_len, KV_LORA),
                  pl.BlockSpec((None, n_kstep, PV_ROWS, KEY_STEP), lambda b, i: (b, 0, 0, 0)),
                  pl.BlockSpec((None, IDX_HEADS, Q_BLOCK), lambda b, i: (b, 0, i)),
                  pl.BlockSpec((Q_BLOCK, n_qidx), lambda b, i: (b * n_qb + i, 0)),
                  pl.BlockSpec((Q_BLOCK, n_lat), lambda b, i: (b * n_qb + i, 0)),
                  pl.BlockSpec(wuvt.shape, lambda b, i: (0, 0, 0))],
        out_specs=pl.BlockSpec((Q_BLOCK, N_HEADS * V_DIM), lambda b, i: (b * n_qb + i, 0)),
        out_shape=jax.ShapeDtypeStruct((n_tok, N_HEADS * V_DIM), BF16),
        scratch_shapes=[pltpu.VMEM((IDX_HEADS * Q_BLOCK, V7X_LANES), BF16),
                        pltpu.VMEM((N_HEADS * Q_BLOCK, KV_LORA + Q_BLOCK), BF16),
                        pltpu.VMEM((seq_len, KV_LORA + Q_BLOCK), BF16),
                        pltpu.VMEM((seq_len, Q_BLOCK), I32),
                        pltpu.VMEM((seq_len, N_HEADS * Q_BLOCK), F32),
                        pltpu.VMEM((1, N_HEADS * Q_BLOCK), F32),
                        pltpu.VMEM((PV_ROWS, N_HEADS * Q_BLOCK), F32),
                        pltpu.VMEM((N_HEADS * V_DIM, Q_BLOCK), F32)],
        compiler_params=_params(("arbitrary", "arbitrary")),
        name="sparse_attn",
    )(kidx, ckv, ckvt, widx_t, qidx, qlat, wuvt)

    c_in = [x2, o, ma, gb, w_attn_out.astype(BF16), w_o.astype(BF16), row(mlp_norm_g),
            w_ff1.astype(BF16), w_ff2.astype(BF16), row(out_norm_g)]
    c_specs = [tok_spec(tm_c, D_MODEL)] * 4 + [_const_spec(a.shape) for a in c_in[4:]]
    out = pl.pallas_call(
        functools.partial(_kernel_c, ff_step=D_MODEL),
        grid=(n_tok // tm_c,),
        in_specs=c_specs,
        out_specs=tok_spec(tm_c, D_MODEL),
        out_shape=jax.ShapeDtypeStruct((n_tok, D_MODEL), F32),
        compiler_params=_params(("arbitrary",)),
        name="block_out",
    )(*c_in)
    return out.reshape(bsz, seq_len, D_MODEL)


def kernel(x, attn_norm_g, w_in, b_gate, dw_w, dw_b, conv_ln_g, conv_ln_b, w_conv_out, q_norm_g,
           kv_norm_g, w_uq, w_uk, w_uv, w_qi, kidx_ln_g, kidx_ln_b, w_attn_out, w_o, mlp_norm_g,
           w_ff1, w_ff2, final_norm_g):
    assert attn_norm_g.shape[0] == 1, "one layer: the final RMSNorm is fused into its output kernel"
    return _layer(x, attn_norm_g[0], w_in[0], b_gate[0], dw_w[0], dw_b[0], conv_ln_g[0],
                  conv_ln_b[0], w_conv_out[0], q_norm_g[0], kv_norm_g[0], w_uq[0], w_uk[0], w_uv[0],
                  w_qi[0], kidx_ln_g[0], kidx_ln_b[0], w_attn_out[0], w_o[0], mlp_norm_g[0],
                  w_ff1[0], w_ff2[0], final_norm_g)
```

```python
import functools

import jax
import jax.numpy as jnp
from jax import lax
from jax.experimental import pallas as pl
from jax.experimental.pallas import tpu as pltpu

D_MODEL = 1024
CHUNK = 64
Q_BLOCK = 128
EPS = 1e-6
CONV_KERNEL = 31
N_HEADS = 16
HEAD_DIM = 64
V_DIM = 64
Q_LORA = 256
KV_LORA = 128
IDX_HEADS = 8
IDX_DIM = 64
IDX_TOPK_MAX = 256
D_FF = 4 * D_MODEL

V7X_LANES = 128
V7X_SUBLANES = 8
V7X_VMEM_LIMIT_BYTES = 60000 * 1024
MXU_COLS = 256

CONV_HALO = 32
CONV_ROWS = 64
NORM_ROWS = 64
KEY_STEP = 4 * Q_BLOCK
HALF_STEP = KEY_STEP // 2
SCORE_ROWS = 2 * Q_BLOCK
NEG_BIG = -1e30
INT_MIN = -(2 ** 31)
LOG2_E = 1.4426950408889634
BISECT_GROUP = 4
V7X_BF16_ROWS = 16
PV_ROWS = KV_LORA + V7X_BF16_ROWS

F32 = jnp.float32
BF16 = jnp.bfloat16
I32 = jnp.int32


def _dot(a, b):
    return jnp.dot(a, b, preferred_element_type=F32)


def _dot_nt(a, b):
    return lax.dot_general(a, b, (((1,), (1,)), ((), ())), preferred_element_type=F32)


def _rms(x, g):
    ms = jnp.mean(x * x, axis=-1, keepdims=True)
    return x * lax.rsqrt(ms + EPS) * g


def _sigmoid(x):
    return 1.0 / (1.0 + jnp.exp(-x))


def _kernel_a(x_ref, ng_ref, wbig_ref, wsm_ref, bgate_ref, dww_ref, dwb_ref, lng_ref, lnb_ref,
              wpw_ref, qng_ref, kvng_ref, wuq_ref, wukbd_ref, wqi_ref, klng_ref, klnb_ref,
              ma_ref, gb_ref, qlat_ref, qidx_ref, ckv_ref, ckvt_ref, kidx_ref, widx_ref,
              vbuf, ybuf, zbuf, gabuf, *, tm, tiles_per_seq):
    t = pl.program_id(0)
    first = (t % tiles_per_seq) == 0

    x = x_ref[...]
    h = _rms(x, ng_ref[...]).astype(BF16)

    n_lt = D_MODEL // V7X_LANES

    @pl.when(first)
    def _():
        vbuf[:, :, 0:CONV_HALO, :] = jnp.zeros((V7X_SUBLANES, n_lt, CONV_HALO, V7X_LANES), F32)

    @pl.when(jnp.logical_not(first))
    def _():
        vbuf[:, :, 0:CONV_HALO, :] = vbuf[:, :, tm:tm + CONV_HALO, :]

    for cb in range(D_MODEL // MXU_COLS):
        c0 = cb * MXU_COLS
        ua = _dot(h, wbig_ref[:, c0:c0 + MXU_COLS])
        ug = _dot(h, wbig_ref[:, D_MODEL + c0:D_MODEL + c0 + MXU_COLS])
        v = ua * _sigmoid(ug)
        for s in range(V7X_SUBLANES):
            for l2 in range(MXU_COLS // V7X_LANES):
                lt = c0 // V7X_LANES + l2
                vbuf[s, lt, CONV_HALO - s:CONV_HALO - s + tm, :] = v[:, l2 * V7X_LANES:(l2 + 1) * V7X_LANES]

    tap0 = CONV_HALO - (CONV_KERNEL - 1)

    row_groups = CONV_ROWS // V7X_SUBLANES

    def conv_chunk(r, carry):
        base = r * CONV_ROWS
        for lt in range(n_lt):
            lanes = slice(lt * V7X_LANES, (lt + 1) * V7X_LANES)
            acc = jnp.zeros((row_groups, V7X_SUBLANES, V7X_LANES), F32)
            for k in range(CONV_KERNEL):
                j, s = divmod(tap0 + k, V7X_SUBLANES)
                row0 = base + V7X_SUBLANES * j
                win = vbuf[s, lt, pl.ds(row0, CONV_ROWS), :]
                acc = acc + dww_ref[k, :, lanes][None] * win.reshape(row_groups, V7X_SUBLANES, V7X_LANES)
            ybuf[pl.ds(base, CONV_ROWS), lanes] = acc.reshape(CONV_ROWS, V7X_LANES)
        return carry

    assert tm // CONV_ROWS == 4 and NORM_ROWS == CONV_ROWS
    def norm_chunk(r, carry):
        base = r * NORM_ROWS
        y = ybuf[pl.ds(base, NORM_ROWS), :] + dwb_ref[...]
        mu = jnp.mean(y, axis=-1, keepdims=True)
        yc = y - mu
        var = jnp.mean(yc * yc, axis=-1, keepdims=True)
        z = yc * lax.rsqrt(var + EPS) * lng_ref[...] + lnb_ref[...]
        z = z * _sigmoid(z)
        zbuf[pl.ds(base, NORM_ROWS), :] = z.astype(BF16)
        return carry

    conv_chunk(0, 0)
    for cb in range(D_MODEL // MXU_COLS):
        c0 = cb * MXU_COLS
        gabuf[:, c0:c0 + MXU_COLS] = _sigmoid(
            _dot(h, wbig_ref[:, 2 * D_MODEL + c0:2 * D_MODEL + c0 + MXU_COLS]) + bgate_ref[:, c0:c0 + MXU_COLS])
    conv_chunk(1, 0)
    norm_chunk(0, 0)
    for cb in range(D_MODEL // MXU_COLS):
        c0 = cb * MXU_COLS
        gb = _sigmoid(_dot(h, wbig_ref[:, 3 * D_MODEL + c0:3 * D_MODEL + c0 + MXU_COLS])
                      + bgate_ref[:, D_MODEL + c0:D_MODEL + c0 + MXU_COLS])
        gb_ref[:, c0:c0 + MXU_COLS] = gb.astype(BF16)

    conv_chunk(2, 0)
    norm_chunk(1, 0)
    us = _dot(h, wsm_ref[...])
    cq = _rms(us[:, 0:Q_LORA], qng_ref[...]).astype(BF16)
    ckv = _rms(us[:, Q_LORA:Q_LORA + KV_LORA], kvng_ref[...])
    ckv_ref[...] = ckv.astype(BF16)
    ckvt_ref[0:KV_LORA, :] = ckv.T.astype(BF16)
    ckvt_ref[KV_LORA:PV_ROWS, :] = jnp.where(
        lax.broadcasted_iota(I32, (PV_ROWS - KV_LORA, tm), 0) == 0, 1.0, 0.0).astype(BF16)

    t3 = us[:, Q_LORA + KV_LORA:Q_LORA + KV_LORA + V7X_LANES]
    lane = lax.broadcasted_iota(I32, t3.shape, 1)
    is_k = lane < IDX_DIM
    mu = jnp.sum(jnp.where(is_k, t3, 0.0), axis=-1, keepdims=True) * (1.0 / IDX_DIM)
    kc = jnp.where(is_k, t3 - mu, 0.0)
    var = jnp.sum(kc * kc, axis=-1, keepdims=True) * (1.0 / IDX_DIM)
    kn = kc * lax.rsqrt(var + EPS) * klng_ref[...] + klnb_ref[...]
    kidx_ref[...] = (kn + pltpu.roll(kn, IDX_DIM, 1)).astype(BF16)
    wi = t3 * (IDX_HEADS ** -0.5 * IDX_DIM ** -0.5)
    widx_ref[...] = wi.T[IDX_DIM:IDX_DIM + IDX_HEADS, :]

    conv_chunk(3, 0)
    norm_chunk(2, 0)
    q = _dot(cq, wuq_ref[...]).astype(BF16)
    heads_per_group = wukbd_ref.shape[1] // HEAD_DIM
    lat_per_group = heads_per_group * KV_LORA
    for g in range(N_HEADS // heads_per_group):
        ql = _dot(q[:, g * heads_per_group * HEAD_DIM:(g + 1) * heads_per_group * HEAD_DIM],
                  wukbd_ref[g])
        qlat_ref[:, g * lat_per_group:(g + 1) * lat_per_group] = (ql * (HEAD_DIM ** -0.5 * LOG2_E)).astype(BF16)
    qidx_ref[...] = _dot(cq, wqi_ref[...]).astype(BF16)

    norm_chunk(3, 0)
    ya = _dot(zbuf[...], wpw_ref[...])
    ma_ref[...] = (gabuf[...] * ya).astype(BF16)


def _kernel_b(kidx_ref, ckv_ref, ckvt_ref, widx_ref, qidx_ref, qlat_ref, wuvt_ref, o_ref,
              qall_scr, qlat_scr, kvb_scr, key_scr, s_scr, m_scr, acc_scr, ot_scr,
              *, seq_len, top_k):
    i = pl.program_id(1)
    n_q = Q_BLOCK
    blk_per_step = KEY_STEP // Q_BLOCK
    rem = (i + 1) % blk_per_step
    has_half = jnp.logical_and(rem > 0, rem <= HALF_STEP // Q_BLOCK)
    n_full = (i + 1) // blk_per_step + jnp.where(rem > HALF_STEP // Q_BLOCK, 1, 0)
    tail0 = pl.multiple_of(n_full * KEY_STEP, KEY_STEP)

    def over_keys(step_fn, init):
        def full(c, carry):
            return step_fn(pl.multiple_of(c * KEY_STEP, KEY_STEP), KEY_STEP, c, carry)
        carry = lax.fori_loop(0, n_full, full, init)
        return lax.cond(has_half, lambda cr: step_fn(tail0, HALF_STEP, n_full, cr), lambda cr: cr, carry)

    lane = lax.broadcasted_iota(I32, (n_q, V7X_LANES), 1)
    for hh in range(IDX_HEADS):
        tile = qidx_ref[:, (hh // 2) * V7X_LANES:(hh // 2 + 1) * V7X_LANES].astype(F32)
        keep = (lane < IDX_DIM) if hh % 2 == 0 else (lane >= IDX_DIM)
        qall_scr[hh * n_q:(hh + 1) * n_q, :] = jnp.where(keep, tile, 0.0).astype(BF16)
    for hh in range(N_HEADS):
        qlat_scr[hh * n_q:(hh + 1) * n_q, 0:KV_LORA] = qlat_ref[:, hh * KV_LORA:(hh + 1) * KV_LORA]

    @pl.when(i == 0)
    def _():
        kvb_scr[:, 0:KV_LORA] = ckv_ref[...]
        eye = (lax.broadcasted_iota(I32, (n_q, n_q), 0) == lax.broadcasted_iota(I32, (n_q, n_q), 1))
        eye = jnp.where(eye, 1.0, 0.0).astype(BF16)
        for hh in range(N_HEADS):
            qlat_scr[hh * n_q:(hh + 1) * n_q, KV_LORA:KV_LORA + n_q] = eye

    qabs_chunk = (i * n_q + lax.broadcasted_iota(I32, (SCORE_ROWS, n_q), 1)) // CHUNK

    def score_step(first, rows, c, carry):
        for sub in range(rows // SCORE_ROWS):
            r0 = pl.multiple_of(first + sub * SCORE_ROWS, SCORE_ROWS)
            kx = kidx_ref[pl.ds(r0, SCORE_ROWS), :]
            sc = jnp.zeros((SCORE_ROWS, n_q), F32)
            for hp in range(IDX_HEADS // 2):
                lg = _dot_nt(kx, qall_scr[hp * MXU_COLS:(hp + 1) * MXU_COLS, :])
                for h2 in range(2):
                    hh = 2 * hp + h2
                    sc = sc + widx_ref[hh:hh + 1, :] * jnp.maximum(lg[:, h2 * n_q:(h2 + 1) * n_q], 0.0)
            bits = pltpu.bitcast(sc, I32)
            key = bits ^ ((bits >> 31) & 0x7FFFFFFF)
            key = jnp.where(bits == INT_MIN, 0, key)
            kabs_chunk = (r0 + lax.broadcasted_iota(I32, (SCORE_ROWS, n_q), 0)) // CHUNK
            key_scr[pl.ds(r0, SCORE_ROWS), :] = jnp.where(kabs_chunk <= qabs_chunk, key, INT_MIN)
        return carry

    over_keys(score_step, 0)

    def count_rows(pred):
        def body(r0, rows, c, cnt):
            hit = pred(key_scr[pl.ds(r0, rows), :], r0)
            ones = jnp.where(hit, 1, 0).astype(I32)
            return cnt + jnp.sum(ones.reshape(rows // V7X_SUBLANES, V7X_SUBLANES, n_q), axis=0)
        cnt8 = over_keys(body, jnp.zeros((V7X_SUBLANES, n_q), I32))
        return jnp.sum(cnt8, axis=0, keepdims=True)

    def bisect_group(state):
        it, cur, settled, _ = state
        for j in range(BISECT_GROUP):
            cand = cur ^ jnp.left_shift(jnp.int32(1), 31 - (it + j))
            cnt = count_rows(lambda k, r0: k >= cand)
            step = jnp.where(cnt >= top_k, cand, cur)
            cur = jnp.where(settled > 0, cur, step)
            settled = jnp.maximum(settled, jnp.where(cnt == top_k, 1, 0))
        open_lanes = n_q - jnp.sum(settled.astype(F32))
        return it + BISECT_GROUP, cur, settled, open_lanes

    def bisect_more(state):
        it, _, _, open_lanes = state
        return jnp.logical_and(it < 32, open_lanes > 0.0)

    _, thr, _, _ = lax.while_loop(
        bisect_more, bisect_group,
        (jnp.int32(0), jnp.full((1, n_q), INT_MIN, I32), jnp.zeros((1, n_q), I32), jnp.float32(n_q)))
    thr = jnp.maximum(thr, INT_MIN + 1)
    n_gt = count_rows(lambda k, r0: k > thr)
    n_ge = count_rows(lambda k, r0: k >= thr)
    need = top_k - n_gt

    def tie_limit():
        n_bits = max(1, (seq_len - 1).bit_length())

        def bisect_row(it, cur):
            cand = cur + jnp.left_shift(jnp.int32(1), n_bits - 1 - it)

            def pred(k, r0):
                row = r0 + lax.broadcasted_iota(I32, k.shape, 0)
                return jnp.logical_and(k == thr, row < cand)
            cnt = count_rows(pred)
            return jnp.where(cnt < need, cand, cur)

        return lax.fori_loop(0, n_bits, bisect_row, jnp.zeros((1, n_q), I32))

    any_cut = jnp.max(n_ge.astype(F32)) > top_k
    row_lim = lax.cond(any_cut, tie_limit, lambda: jnp.full((1, n_q), seq_len, I32))

    def bias_chunk(r0, rows, c, carry):
        k = key_scr[pl.ds(r0, rows), :]
        row = r0 + lax.broadcasted_iota(I32, (rows, n_q), 0)
        take = jnp.logical_or(k > thr, jnp.logical_and(k == thr, row <= row_lim))
        kvb_scr[pl.ds(r0, rows), KV_LORA:KV_LORA + n_q] = jnp.where(take, 0.0, NEG_BIG).astype(BF16)
        return carry

    over_keys(bias_chunk, 0)

    pair = 2 * n_q
    n_pair = N_HEADS // 2
    n_col = N_HEADS * n_q

    def qk_step(r0, rows, c, m8):
        kvb = kvb_scr[pl.ds(r0, rows), :]
        parts = []
        for hp in range(n_pair):
            s = _dot_nt(kvb, qlat_scr[hp * pair:(hp + 1) * pair, :])
            s_scr[pl.ds(r0, rows), hp * pair:(hp + 1) * pair] = s
            parts.append(jnp.max(s.reshape(rows // V7X_SUBLANES, V7X_SUBLANES, pair), axis=0))
        return jnp.maximum(m8, jnp.concatenate(parts, axis=1))

    m8 = over_keys(qk_step, jnp.full((V7X_SUBLANES, n_col), NEG_BIG, F32))
    m_scr[...] = jnp.max(m8, axis=0, keepdims=True)

    acc_scr[...] = jnp.zeros(acc_scr.shape, F32)

    def pv_step(r0, rows, c, carry):
        kvt = ckvt_ref[c, :, 0:rows]
        for hp in range(n_pair):
            cols = slice(hp * pair, (hp + 1) * pair)
            p = jnp.exp2(s_scr[pl.ds(r0, rows), cols] - m_scr[:, cols])
            acc_scr[:, cols] += _dot(kvt, p.astype(BF16))
        return carry

    over_keys(pv_step, 0)

    for hh in range(N_HEADS):
        cols = slice(hh * n_q, (hh + 1) * n_q)
        inv_l = 1.0 / acc_scr[KV_LORA:KV_LORA + 1, cols]
        olat = (acc_scr[0:KV_LORA, cols] * inv_l).astype(BF16)
        ot_scr[hh * V_DIM:(hh + 1) * V_DIM, :] = _dot(wuvt_ref[hh], olat)
    o_ref[...] = ot_scr[...].T.astype(BF16)


def _kernel_c(x_ref, o_ref, ma_ref, gb_ref, wao_ref, wo_ref, mng_ref, w1_ref, w2_ref, fng_ref,
              out_ref, *, ff_step):
    yb = _dot(o_ref[...], wao_ref[...])
    merged = ma_ref[...].astype(F32) + gb_ref[...].astype(F32) * yb
    x1 = x_ref[...] + _dot(merged.astype(BF16), wo_ref[...])
    hm = _rms(x1, mng_ref[...]).astype(BF16)
    acc = x1
    for j in range(D_FF // ff_step):
        h1 = jnp.maximum(_dot(hm, w1_ref[:, j * ff_step:(j + 1) * ff_step]), 0.0)
        acc = acc + _dot((h1 * h1).astype(BF16), w2_ref[j * ff_step:(j + 1) * ff_step, :])
    out_ref[...] = _rms(acc, fng_ref[...])


def _const_spec(shape):
    nd = len(shape)
    return pl.BlockSpec(shape, lambda *_: (0,) * nd, pipeline_mode=pl.Buffered(1))


def _params(semantics):
    return pltpu.CompilerParams(dimension_semantics=semantics,
                                vmem_limit_bytes=V7X_VMEM_LIMIT_BYTES)


def _layer(x, attn_norm_g, w_in, b_gate, dw_w, dw_b, conv_ln_g, conv_ln_b, w_conv_out, q_norm_g,
           kv_norm_g, w_uq, w_uk, w_uv, w_qi, kidx_ln_g, kidx_ln_b, w_attn_out, w_o, mlp_norm_g,
           w_ff1, w_ff2, out_norm_g):
    bsz, seq_len, _ = x.shape
    n_tok = bsz * seq_len
    top_k = min(IDX_TOPK_MAX, seq_len // 4)
    tm_a = min(256, seq_len)
    tm_c = min(512, seq_len)
    assert seq_len % Q_BLOCK == 0 and seq_len % tm_a == 0 and seq_len % tm_c == 0
    assert seq_len % KEY_STEP == 0

    row = lambda v: v.reshape(1, -1).astype(F32)
    x2 = x.reshape(n_tok, D_MODEL)

    o_cq = 2 * D_MODEL
    o_ckv = o_cq + Q_LORA
    o_kidx = o_ckv + KV_LORA
    o_widx = o_kidx + IDX_DIM
    o_gate = o_widx + IDX_HEADS
    wbig = jnp.concatenate([w_in[:, 0:2 * D_MODEL], w_in[:, o_gate:o_gate + 2 * D_MODEL]],
                           axis=1).astype(BF16)
    n_small = o_gate - o_cq
    wsm = jnp.pad(w_in[:, o_cq:o_gate], ((0, 0), (0, 4 * V7X_LANES - n_small))).astype(BF16)

    dww = jnp.broadcast_to(dw_w.astype(F32)[:, None, :], (CONV_KERNEL, V7X_SUBLANES, D_MODEL))
    klng = jnp.pad(kidx_ln_g, (0, V7X_LANES - IDX_DIM)).reshape(1, -1).astype(F32)
    klnb = jnp.pad(kidx_ln_b, (0, V7X_LANES - IDX_DIM)).reshape(1, -1).astype(F32)

    hpg = 4
    wuk_g = w_uk.reshape(N_HEADS // hpg, hpg, HEAD_DIM, KV_LORA)
    eye = jnp.eye(hpg, dtype=w_uk.dtype)
    wukbd = jnp.einsum("ghdc,hk->ghdkc", wuk_g, eye).reshape(
        N_HEADS // hpg, hpg * HEAD_DIM, hpg * KV_LORA).astype(BF16)

    tiles_per_seq = seq_len // tm_a
    tok_spec = lambda tm, n: pl.BlockSpec((tm, n), lambda t: (t, 0))
    a_in = [x2, row(attn_norm_g), wbig, wsm, row(b_gate), dww, row(dw_b), row(conv_ln_g),
            row(conv_ln_b), w_conv_out.astype(BF16), row(q_norm_g), row(kv_norm_g),
            w_uq.astype(BF16), wukbd, w_qi.astype(BF16), klng, klnb]
    a_specs = [tok_spec(tm_a, D_MODEL)] + [_const_spec(a.shape) for a in a_in[1:]]
    n_lat = N_HEADS * KV_LORA
    n_qidx = IDX_HEADS * IDX_DIM
    n_kstep = seq_len // KEY_STEP
    tiles_per_kstep = KEY_STEP // tm_a
    ma, gb, qlat, qidx, ckv, ckvt, kidx, widx_t = pl.pallas_call(
        functools.partial(_kernel_a, tm=tm_a, tiles_per_seq=tiles_per_seq),
        grid=(n_tok // tm_a,),
        in_specs=a_specs,
        out_specs=[tok_spec(tm_a, D_MODEL), tok_spec(tm_a, D_MODEL), tok_spec(tm_a, n_lat),
                   tok_spec(tm_a, n_qidx), tok_spec(tm_a, KV_LORA),
                   pl.BlockSpec((None, None, PV_ROWS, tm_a),
                                lambda t: (t // tiles_per_seq, (t % tiles_per_seq) // tiles_per_kstep, 0,
                                           t % tiles_per_kstep)),
                   tok_spec(tm_a, V7X_LANES),
                   pl.BlockSpec((None, IDX_HEADS, tm_a), lambda t: (t // tiles_per_seq, 0, t % tiles_per_seq))],
        out_shape=[jax.ShapeDtypeStruct((n_tok, D_MODEL), BF16),
                   jax.ShapeDtypeStruct((n_tok, D_MODEL), BF16),
                   jax.ShapeDtypeStruct((n_tok, n_lat), BF16),
                   jax.ShapeDtypeStruct((n_tok, n_qidx), BF16),
                   jax.ShapeDtypeStruct((n_tok, KV_LORA), BF16),
                   jax.ShapeDtypeStruct((bsz, n_kstep, PV_ROWS, KEY_STEP), BF16),
                   jax.ShapeDtypeStruct((n_tok, V7X_LANES), BF16),
                   jax.ShapeDtypeStruct((bsz, IDX_HEADS, seq_len), F32)],
        scratch_shapes=[pltpu.VMEM((V7X_SUBLANES, D_MODEL // V7X_LANES, CONV_HALO + tm_a, V7X_LANES), F32),
                        pltpu.VMEM((tm_a, D_MODEL), F32),
                        pltpu.VMEM((tm_a, D_MODEL), BF16),
                        pltpu.VMEM((tm_a, D_MODEL), F32)],
        compiler_params=_params(("arbitrary",)),
        name="block_in",
    )(*a_in)

    ckv = ckv.reshape(bsz, seq_len, KV_LORA)
    kidx = kidx.reshape(bsz, seq_len, V7X_LANES)
    wuvt = jnp.swapaxes(w_uv, 1, 2).astype(BF16)

    n_qb = seq_len // Q_BLOCK
    seq_spec = lambda r, c: pl.BlockSpec((None, r, c), lambda b, i: (b, 0, 0))
    o = pl.pallas_call(
        functools.partial(_kernel_b, seq_len=seq_len, top_k=top_k),
        grid=(bsz, n_qb),
        in_specs=[seq_spec(seq_len, V7X_LANES), seq_spec(seq_len, KV_LORA),
                  pl.BlockSpec((None, n_kstep, PV_ROWS, KEY_STEP), lambda b, i: (b, 0, 0, 0)),
                  pl.BlockSpec((None, IDX_HEADS, Q_BLOCK), lambda b, i: (b, 0, i)),
                  pl.BlockSpec((Q_BLOCK, n_qidx), lambda b, i: (b * n_qb + i, 0)),
                  pl.BlockSpec((Q_BLOCK, n_lat), lambda b, i: (b * n_qb + i, 0)),
                  pl.BlockSpec(wuvt.shape, lambda b, i: (0, 0, 0))],
        out_specs=pl.BlockSpec((Q_BLOCK, N_HEADS * V_DIM), lambda b, i: (b * n_qb + i, 0)),
        out_shape=jax.ShapeDtypeStruct((n_tok, N_HEADS * V_DIM), BF16),
        scratch_shapes=[pltpu.VMEM((IDX_HEADS * Q_BLOCK, V7X_LANES), BF16),
                        pltpu.VMEM((N_HEADS * Q_BLOCK, KV_LORA + Q_BLOCK), BF16),
                        pltpu.VMEM((seq_len, KV_LORA + Q_BLOCK), BF16),
                        pltpu.VMEM((seq_len, Q_BLOCK), I32),
                        pltpu.VMEM((seq_len, N_HEADS * Q_BLOCK), F32),
                        pltpu.VMEM((1, N_HEADS * Q_BLOCK), F32),
                        pltpu.VMEM((PV_ROWS, N_HEADS * Q_BLOCK), F32),
                        pltpu.VMEM((N_HEADS * V_DIM, Q_BLOCK), F32)],
        compiler_params=_params(("arbitrary", "arbitrary")),
        name="sparse_attn",
    )(kidx, ckv, ckvt, widx_t, qidx, qlat, wuvt)

    c_in = [x2, o, ma, gb, w_attn_out.astype(BF16), w_o.astype(BF16), row(mlp_norm_g),
            w_ff1.astype(BF16), w_ff2.astype(BF16), row(out_norm_g)]
    c_specs = [tok_spec(tm_c, D_MODEL)] * 4 + [_const_spec(a.shape) for a in c_in[4:]]
    out = pl.pallas_call(
        functools.partial(_kernel_c, ff_step=D_MODEL),
        grid=(n_tok // tm_c,),
        in_specs=c_specs,
        out_specs=tok_spec(tm_c, D_MODEL),
        out_shape=jax.ShapeDtypeStruct((n_tok, D_MODEL), F32),
        compiler_params=_params(("arbitrary",)),
        name="block_out",
    )(*c_in)
    return out.reshape(bsz, seq_len, D_MODEL)


def kernel(x, attn_norm_g, w_in, b_gate, dw_w, dw_b, conv_ln_g, conv_ln_b, w_conv_out, q_norm_g,
           kv_norm_g, w_uq, w_uk, w_uv, w_qi, kidx_ln_g, kidx_ln_b, w_attn_out, w_o, mlp_norm_g,
           w_ff1, w_ff2, final_norm_g):
    assert attn_norm_g.shape[0] == 1, "one layer: the final RMSNorm is fused into its output kernel"
    return _layer(x, attn_norm_g[0], w_in[0], b_gate[0], dw_w[0], dw_b[0], conv_ln_g[0],
                  conv_ln_b[0], w_conv_out[0], q_norm_g[0], kv_norm_g[0], w_uq[0], w_uk[0], w_uv[0],
                  w_qi[0], kidx_ln_g[0], kidx_ln_b[0], w_attn_out[0], w_o[0], mlp_norm_g[0],
                  w_ff1[0], w_ff2[0], final_norm_g)
```

```python
import functools

import jax
import jax.numpy as jnp
from jax import lax
from jax.experimental import pallas as pl
from jax.experimental.pallas import tpu as pltpu

D_MODEL = 1024
CHUNK = 64
Q_BLOCK = 128
EPS = 1e-6
CONV_KERNEL = 31
N_HEADS = 16
HEAD_DIM = 64
V_DIM = 64
Q_LORA = 256
KV_LORA = 128
IDX_HEADS = 8
IDX_DIM = 64
IDX_TOPK_MAX = 256
D_FF = 4 * D_MODEL

V7X_LANES = 128
V7X_SUBLANES = 8
V7X_VMEM_LIMIT_BYTES = 60000 * 1024
MXU_COLS = 256

CONV_HALO = 32
CONV_ROWS = 64
NORM_ROWS = 64
KEY_STEP = 4 * Q_BLOCK
HALF_STEP = KEY_STEP // 2
SCORE_ROWS = 2 * Q_BLOCK
NEG_BIG = -1e30
INT_MIN = -(2 ** 31)
LOG2_E = 1.4426950408889634
BISECT_GROUP = 4
V7X_BF16_ROWS = 16
PV_ROWS = KV_LORA + V7X_BF16_ROWS

F32 = jnp.float32
BF16 = jnp.bfloat16
I32 = jnp.int32


def _dot(a, b):
    return jnp.dot(a, b, preferred_element_type=F32)


def _dot_nt(a, b):
    return lax.dot_general(a, b, (((1,), (1,)), ((), ())), preferred_element_type=F32)


def _rms(x, g):
    ms = jnp.mean(x * x, axis=-1, keepdims=True)
    return x * lax.rsqrt(ms + EPS) * g


def _sigmoid(x):
    return 1.0 / (1.0 + jnp.exp(-x))


def _kernel_a(x_ref, ng_ref, wbig_ref, wsm_ref, bgate_ref, dww_ref, dwb_ref, lng_ref, lnb_ref,
              wpw_ref, qng_ref, kvng_ref, wuq_ref, wukbd_ref, wqi_ref, klng_ref, klnb_ref,
              ma_ref, gb_ref, qlat_ref, qidx_ref, ckv_ref, ckvt_ref, kidx_ref, widx_ref,
              vbuf, ybuf, zbuf, gabuf, *, tm, tiles_per_seq):
    t = pl.program_id(0)
    first = (t % tiles_per_seq) == 0

    x = x_ref[...]
    h = _rms(x, ng_ref[...]).astype(BF16)

    n_lt = D_MODEL // V7X_LANES

    @pl.when(first)
    def _():
        vbuf[:, :, 0:CONV_HALO, :] = jnp.zeros((V7X_SUBLANES, n_lt, CONV_HALO, V7X_LANES), F32)

    @pl.when(jnp.logical_not(first))
    def _():
        vbuf[:, :, 0:CONV_HALO, :] = vbuf[:, :, tm:tm + CONV_HALO, :]

    for cb in range(D_MODEL // MXU_COLS):
        c0 = cb * MXU_COLS
        ua = _dot(h, wbig_ref[:, c0:c0 + MXU_COLS])
        ug = _dot(h, wbig_ref[:, D_MODEL + c0:D_MODEL + c0 + MXU_COLS])
        v = ua * _sigmoid(ug)
        for s in range(V7X_SUBLANES):
            for l2 in range(MXU_COLS // V7X_LANES):
                lt = c0 // V7X_LANES + l2
                vbuf[s, lt, CONV_HALO - s:CONV_HALO - s + tm, :] = v[:, l2 * V7X_LANES:(l2 + 1) * V7X_LANES]

    tap0 = CONV_HALO - (CONV_KERNEL - 1)

    row_groups = CONV_ROWS // V7X_SUBLANES

    def conv_chunk(r, carry):
        base = r * CONV_ROWS
        for lt in range(n_lt):
            lanes = slice(lt * V7X_LANES, (lt + 1) * V7X_LANES)
            acc = jnp.zeros((row_groups, V7X_SUBLANES, V7X_LANES), F32)
            for k in range(CONV_KERNEL):
                j, s = divmod(tap0 + k, V7X_SUBLANES)
                row0 = base + V7X_SUBLANES * j
                win = vbuf[s, lt, pl.ds(row0, CONV_ROWS), :]
                acc = acc + dww_ref[k, :, lanes][None] * win.reshape(row_groups, V7X_SUBLANES, V7X_LANES)
            ybuf[pl.ds(base, CONV_ROWS), lanes] = acc.reshape(CONV_ROWS, V7X_LANES)
        return carry

    assert tm // CONV_ROWS == 4 and NORM_ROWS == CONV_ROWS
    def norm_chunk(r, carry):
        base = r * NORM_ROWS
        y = ybuf[pl.ds(base, NORM_ROWS), :] + dwb_ref[...]
        mu = jnp.mean(y, axis=-1, keepdims=True)
        yc = y - mu
        var = jnp.mean(yc * yc, axis=-1, keepdims=True)
        z = yc * lax.rsqrt(var + EPS) * lng_ref[...] + lnb_ref[...]
        z = z * _sigmoid(z)
        zbuf[pl.ds(base, NORM_ROWS), :] = z.astype(BF16)
        return carry

    conv_chunk(0, 0)
    for cb in range(D_MODEL // MXU_COLS):
        c0 = cb * MXU_COLS
        gabuf[:, c0:c0 + MXU_COLS] = _sigmoid(
            _dot(h, wbig_ref[:, 2 * D_MODEL + c0:2 * D_MODEL + c0 + MXU_COLS]) + bgate_ref[:, c0:c0 + MXU_COLS])
    conv_chunk(1, 0)
    norm_chunk(0, 0)
    for cb in range(D_MODEL // MXU_COLS):
        c0 = cb * MXU_COLS
        gb = _sigmoid(_dot(h, wbig_ref[:, 3 * D_MODEL + c0:3 * D_MODEL + c0 + MXU_COLS])
                      + bgate_ref[:, D_MODEL + c0:D_MODEL + c0 + MXU_COLS])
        gb_ref[:, c0:c0 + MXU_COLS] = gb.astype(BF16)

    conv_chunk(2, 0)
    norm_chunk(1, 0)
    us = _dot(h, wsm_ref[...])
    cq = _rms(us[:, 0:Q_LORA], qng_ref[...]).astype(BF16)
    ckv = _rms(us[:, Q_LORA:Q_LORA + KV_LORA], kvng_ref[...])
    ckv_ref[...] = ckv.astype(BF16)
    ckvt_ref[0:KV_LORA, :] = ckv.T.astype(BF16)
    ckvt_ref[KV_LORA:PV_ROWS, :] = jnp.where(
        lax.broadcasted_iota(I32, (PV_ROWS - KV_LORA, tm), 0) == 0, 1.0, 0.0).astype(BF16)

    t3 = us[:, Q_LORA + KV_LORA:Q_LORA + KV_LORA + V7X_LANES]
    lane = lax.broadcasted_iota(I32, t3.shape, 1)
    is_k = lane < IDX_DIM
    mu = jnp.sum(jnp.where(is_k, t3, 0.0), axis=-1, keepdims=True) * (1.0 / IDX_DIM)
    kc = jnp.where(is_k, t3 - mu, 0.0)
    var = jnp.sum(kc * kc, axis=-1, keepdims=True) * (1.0 / IDX_DIM)
    kn = kc * lax.rsqrt(var + EPS) * klng_ref[...] + klnb_ref[...]
    kidx_ref[...] = (kn + pltpu.roll(kn, IDX_DIM, 1)).astype(BF16)
    wi = t3 * (IDX_HEADS ** -0.5 * IDX_DIM ** -0.5)
    widx_ref[...] = wi.T[IDX_DIM:IDX_DIM + IDX_HEADS, :]

    conv_chunk(3, 0)
    norm_chunk(2, 0)
    q = _dot(cq, wuq_ref[...]).astype(BF16)
    heads_per_group = wukbd_ref.shape[1] // HEAD_DIM
    lat_per_group = heads_per_group * KV_LORA
    for g in range(N_HEADS // heads_per_group):
        ql = _dot(q[:, g * heads_per_group * HEAD_DIM:(g + 1) * heads_per_group * HEAD_DIM],
                  wukbd_ref[g])
        qlat_ref[:, g * lat_per_group:(g + 1) * lat_per_group] = (ql * (HEAD_DIM ** -0.5 * LOG2_E)).astype(BF16)
    qidx_ref[...] = _dot(cq, wqi_ref[...]).astype(BF16)

    norm_chunk(3, 0)
    ya = _dot(zbuf[...], wpw_ref[...])
    ma_ref[...] = (gabuf[...] * ya).astype(BF16)


def _kernel_b(kidx_ref, ckv_ref, ckvt_ref, widx_ref, qidx_ref, qlat_ref, wuvt_ref, o_ref,
              qall_scr, qlat_scr, key_scr, s_scr, m_scr, acc_scr, ot_scr,
              *, seq_len, top_k):
    i = pl.program_id(1)
    n_q = Q_BLOCK
    blk_per_step = KEY_STEP // Q_BLOCK
    rem = (i + 1) % blk_per_step
    has_half = jnp.logical_and(rem > 0, rem <= HALF_STEP // Q_BLOCK)
    n_full = (i + 1) // blk_per_step + jnp.where(rem > HALF_STEP // Q_BLOCK, 1, 0)
    tail0 = pl.multiple_of(n_full * KEY_STEP, KEY_STEP)

    def over_keys(step_fn, init):
        def full(c, carry):
            return step_fn(pl.multiple_of(c * KEY_STEP, KEY_STEP), KEY_STEP, c, carry)
        carry = lax.fori_loop(0, n_full, full, init)
        return lax.cond(has_half, lambda cr: step_fn(tail0, HALF_STEP, n_full, cr), lambda cr: cr, carry)

    lane = lax.broadcasted_iota(I32, (n_q, V7X_LANES), 1)
    for hh in range(IDX_HEADS):
        tile = qidx_ref[:, (hh // 2) * V7X_LANES:(hh // 2 + 1) * V7X_LANES].astype(F32)
        keep = (lane < IDX_DIM) if hh % 2 == 0 else (lane >= IDX_DIM)
        qall_scr[hh * n_q:(hh + 1) * n_q, :] = jnp.where(keep, tile, 0.0).astype(BF16)
    for hh in range(N_HEADS):
        qlat_scr[hh * n_q:(hh + 1) * n_q, 0:KV_LORA] = qlat_ref[:, hh * KV_LORA:(hh + 1) * KV_LORA]

    @pl.when(i == 0)
    def _():
        eye = (lax.broadcasted_iota(I32, (n_q, n_q), 0) == lax.broadcasted_iota(I32, (n_q, n_q), 1))
        eye = jnp.where(eye, 1.0, 0.0).astype(BF16)
        for hh in range(N_HEADS):
            qlat_scr[hh * n_q:(hh + 1) * n_q, KV_LORA:KV_LORA + n_q] = eye

    qabs_chunk = (i * n_q + lax.broadcasted_iota(I32, (SCORE_ROWS, n_q), 1)) // CHUNK

    def score_step(first, rows, c, carry):
        for sub in range(rows // SCORE_ROWS):
            r0 = pl.multiple_of(first + sub * SCORE_ROWS, SCORE_ROWS)
            kx = kidx_ref[pl.ds(r0, SCORE_ROWS), :]
            sc = jnp.zeros((SCORE_ROWS, n_q), F32)
            for hp in range(IDX_HEADS // 2):
                lg = _dot_nt(kx, qall_scr[hp * MXU_COLS:(hp + 1) * MXU_COLS, :])
                for h2 in range(2):
                    hh = 2 * hp + h2
                    sc = sc + widx_ref[hh:hh + 1, :] * jnp.maximum(lg[:, h2 * n_q:(h2 + 1) * n_q], 0.0)
            bits = pltpu.bitcast(sc, I32)
            key = bits ^ ((bits >> 31) & 0x7FFFFFFF)
            key = jnp.where(bits == INT_MIN, 0, key)
            kabs_chunk = (r0 + lax.broadcasted_iota(I32, (SCORE_ROWS, n_q), 0)) // CHUNK
            key_scr[pl.ds(r0, SCORE_ROWS), :] = jnp.where(kabs_chunk <= qabs_chunk, key, INT_MIN)
        return carry

    over_keys(score_step, 0)

    def count_rows(pred):
        def body(r0, rows, c, cnt):
            hit = pred(key_scr[pl.ds(r0, rows), :], r0)
            ones = jnp.where(hit, 1, 0).astype(I32)
            return cnt + jnp.sum(ones.reshape(rows // V7X_SUBLANES, V7X_SUBLANES, n_q), axis=0)
        cnt8 = over_keys(body, jnp.zeros((V7X_SUBLANES, n_q), I32))
        return jnp.sum(cnt8, axis=0, keepdims=True)

    def bisect_group(state):
        it, cur, settled, _ = state
        for j in range(BISECT_GROUP):
            cand = cur ^ jnp.left_shift(jnp.int32(1), 31 - (it + j))
            cnt = count_rows(lambda k, r0: k >= cand)
            step = jnp.where(cnt >= top_k, cand, cur)
            cur = jnp.where(settled > 0, cur, step)
            settled = jnp.maximum(settled, jnp.where(cnt == top_k, 1, 0))
        open_lanes = n_q - jnp.sum(settled.astype(F32))
        return it + BISECT_GROUP, cur, settled, open_lanes

    def bisect_more(state):
        it, _, _, open_lanes = state
        return jnp.logical_and(it < 32, open_lanes > 0.0)

    _, thr, _, open_lanes = lax.while_loop(
        bisect_more, bisect_group,
        (jnp.int32(0), jnp.full((1, n_q), INT_MIN, I32), jnp.zeros((1, n_q), I32), jnp.float32(n_q)))
    thr = jnp.maximum(thr, INT_MIN + 1)
    all_rows = lambda: jnp.full((1, n_q), seq_len, I32)

    def tie_rows():
        n_gt = count_rows(lambda k, r0: k > thr)
        n_ge = count_rows(lambda k, r0: k >= thr)
        need = top_k - n_gt

        def tie_limit():
            n_bits = max(1, (seq_len - 1).bit_length())

            def bisect_row(it, cur):
                cand = cur + jnp.left_shift(jnp.int32(1), n_bits - 1 - it)

                def pred(k, r0):
                    row = r0 + lax.broadcasted_iota(I32, k.shape, 0)
                    return jnp.logical_and(k == thr, row < cand)
                cnt = count_rows(pred)
                return jnp.where(cnt < need, cand, cur)

            return lax.fori_loop(0, n_bits, bisect_row, jnp.zeros((1, n_q), I32))

        return lax.cond(jnp.max(n_ge.astype(F32)) > top_k, tie_limit, all_rows)

    row_lim = lax.cond(open_lanes > 0.0, tie_rows, all_rows)

    pair = 2 * n_q
    n_pair = N_HEADS // 2
    n_col = N_HEADS * n_q

    def qk_step(r0, rows, c, m8):
        k = key_scr[pl.ds(r0, rows), :]
        row = r0 + lax.broadcasted_iota(I32, (rows, n_q), 0)
        take = jnp.logical_or(k > thr, jnp.logical_and(k == thr, row <= row_lim))
        mask = jnp.where(take, 0.0, NEG_BIG).astype(BF16)
        kvb = jnp.concatenate([ckv_ref[pl.ds(r0, rows), :], mask], axis=1)
        parts = []
        for hp in range(n_pair):
            s = _dot_nt(kvb, qlat_scr[hp * pair:(hp + 1) * pair, :])
            s_scr[pl.ds(r0, rows), hp * pair:(hp + 1) * pair] = s
            parts.append(jnp.max(s.reshape(rows // V7X_SUBLANES, V7X_SUBLANES, pair), axis=0))
        return jnp.maximum(m8, jnp.concatenate(parts, axis=1))

    m8 = over_keys(qk_step, jnp.full((V7X_SUBLANES, n_col), NEG_BIG, F32))
    m_scr[...] = jnp.max(m8, axis=0, keepdims=True)

    acc_scr[...] = jnp.zeros(acc_scr.shape, F32)

    def pv_step(r0, rows, c, carry):
        kvt = ckvt_ref[c, :, 0:rows]
        for hp in range(n_pair):
            cols = slice(hp * pair, (hp + 1) * pair)
            p = jnp.exp2(s_scr[pl.ds(r0, rows), cols] - m_scr[:, cols])
            acc_scr[:, cols] += _dot(kvt, p.astype(BF16))
        return carry

    over_keys(pv_step, 0)

    for hh in range(N_HEADS):
        cols = slice(hh * n_q, (hh + 1) * n_q)
        inv_l = 1.0 / acc_scr[KV_LORA:KV_LORA + 1, cols]
        olat = (acc_scr[0:KV_LORA, cols] * inv_l).astype(BF16)
        ot_scr[hh * V_DIM:(hh + 1) * V_DIM, :] = _dot(wuvt_ref[hh], olat)
    o_ref[...] = ot_scr[...].T.astype(BF16)


def _kernel_c(x_ref, o_ref, ma_ref, gb_ref, wao_ref, wo_ref, mng_ref, w1_ref, w2_ref, fng_ref,
              out_ref, *, ff_step):
    yb = _dot(o_ref[...], wao_ref[...])
    merged = ma_ref[...].astype(F32) + gb_ref[...].astype(F32) * yb
    x1 = x_ref[...] + _dot(merged.astype(BF16), wo_ref[...])
    hm = _rms(x1, mng_ref[...]).astype(BF16)
    acc = x1
    for j in range(D_FF // ff_step):
        h1 = jnp.maximum(_dot(hm, w1_ref[:, j * ff_step:(j + 1) * ff_step]), 0.0)
        acc = acc + _dot((h1 * h1).astype(BF16), w2_ref[j * ff_step:(j + 1) * ff_step, :])
    out_ref[...] = _rms(acc, fng_ref[...])


def _const_spec(shape):
    nd = len(shape)
    return pl.BlockSpec(shape, lambda *_: (0,) * nd, pipeline_mode=pl.Buffered(1))


def _params(semantics):
    return pltpu.CompilerParams(dimension_semantics=semantics,
                                vmem_limit_bytes=V7X_VMEM_LIMIT_BYTES)


def _layer(x, attn_norm_g, w_in, b_gate, dw_w, dw_b, conv_ln_g, conv_ln_b, w_conv_out, q_norm_g,
           kv_norm_g, w_uq, w_uk, w_uv, w_qi, kidx_ln_g, kidx_ln_b, w_attn_out, w_o, mlp_norm_g,
           w_ff1, w_ff2, out_norm_g):
    bsz, seq_len, _ = x.shape
    n_tok = bsz * seq_len
    top_k = min(IDX_TOPK_MAX, seq_len // 4)
    tm_a = min(256, seq_len)
    tm_c = min(512, seq_len)
    assert seq_len % Q_BLOCK == 0 and seq_len % tm_a == 0 and seq_len % tm_c == 0
    assert seq_len % KEY_STEP == 0

    row = lambda v: v.reshape(1, -1).astype(F32)
    x2 = x.reshape(n_tok, D_MODEL)

    o_cq = 2 * D_MODEL
    o_ckv = o_cq + Q_LORA
    o_kidx = o_ckv + KV_LORA
    o_widx = o_kidx + IDX_DIM
    o_gate = o_widx + IDX_HEADS
    wbig = jnp.concatenate([w_in[:, 0:2 * D_MODEL], w_in[:, o_gate:o_gate + 2 * D_MODEL]],
                           axis=1).astype(BF16)
    n_small = o_gate - o_cq
    wsm = jnp.pad(w_in[:, o_cq:o_gate], ((0, 0), (0, 4 * V7X_LANES - n_small))).astype(BF16)

    dww = jnp.broadcast_to(dw_w.astype(F32)[:, None, :], (CONV_KERNEL, V7X_SUBLANES, D_MODEL))
    klng = jnp.pad(kidx_ln_g, (0, V7X_LANES - IDX_DIM)).reshape(1, -1).astype(F32)
    klnb = jnp.pad(kidx_ln_b, (0, V7X_LANES - IDX_DIM)).reshape(1, -1).astype(F32)

    hpg = 4
    wuk_g = w_uk.reshape(N_HEADS // hpg, hpg, HEAD_DIM, KV_LORA)
    eye = jnp.eye(hpg, dtype=w_uk.dtype)
    wukbd = jnp.einsum("ghdc,hk->ghdkc", wuk_g, eye).reshape(
        N_HEADS // hpg, hpg * HEAD_DIM, hpg * KV_LORA).astype(BF16)

    tiles_per_seq = seq_len // tm_a
    tok_spec = lambda tm, n: pl.BlockSpec((tm, n), lambda t: (t, 0))
    a_in = [x2, row(attn_norm_g), wbig, wsm, row(b_gate), dww, row(dw_b), row(conv_ln_g),
            row(conv_ln_b), w_conv_out.astype(BF16), row(q_norm_g), row(kv_norm_g),
            w_uq.astype(BF16), wukbd, w_qi.astype(BF16), klng, klnb]
    a_specs = [tok_spec(tm_a, D_MODEL)] + [_const_spec(a.shape) for a in a_in[1:]]
    n_lat = N_HEADS * KV_LORA
    n_qidx = IDX_HEADS * IDX_DIM
    n_kstep = seq_len // KEY_STEP
    tiles_per_kstep = KEY_STEP // tm_a
    ma, gb, qlat, qidx, ckv, ckvt, kidx, widx_t = pl.pallas_call(
        functools.partial(_kernel_a, tm=tm_a, tiles_per_seq=tiles_per_seq),
        grid=(n_tok // tm_a,),
        in_specs=a_specs,
        out_specs=[tok_spec(tm_a, D_MODEL), tok_spec(tm_a, D_MODEL), tok_spec(tm_a, n_lat),
                   tok_spec(tm_a, n_qidx), tok_spec(tm_a, KV_LORA),
                   pl.BlockSpec((None, None, PV_ROWS, tm_a),
                                lambda t: (t // tiles_per_seq, (t % tiles_per_seq) // tiles_per_kstep, 0,
                                           t % tiles_per_kstep)),
                   tok_spec(tm_a, V7X_LANES),
                   pl.BlockSpec((None, IDX_HEADS, tm_a), lambda t: (t // tiles_per_seq, 0, t % tiles_per_seq))],
        out_shape=[jax.ShapeDtypeStruct((n_tok, D_MODEL), BF16),
                   jax.ShapeDtypeStruct((n_tok, D_MODEL), BF16),
                   jax.ShapeDtypeStruct((n_tok, n_lat), BF16),
                   jax.ShapeDtypeStruct((n_tok, n_qidx), BF16),
                   jax.ShapeDtypeStruct((n_tok, KV_LORA), BF16),
                   jax.ShapeDtypeStruct((bsz, n_kstep, PV_ROWS, KEY_STEP), BF16),
                   jax.ShapeDtypeStruct((n_tok, V7X_LANES), BF16),
                   jax.ShapeDtypeStruct((bsz, IDX_HEADS, seq_len), F32)],
        scratch_shapes=[pltpu.VMEM((V7X_SUBLANES, D_MODEL // V7X_LANES, CONV_HALO + tm_a, V7X_LANES), F32),
                        pltpu.VMEM((tm_a, D_MODEL), F32),
                        pltpu.VMEM((tm_a, D_MODEL), BF16),
                        pltpu.VMEM((tm_a, D_MODEL), F32)],
        compiler_params=_params(("arbitrary",)),
        name="block_in",
    )(*a_in)

    ckv = ckv.reshape(bsz, seq_len, KV_LORA)
    kidx = kidx.reshape(bsz, seq_len, V7X_LANES)
    wuvt = jnp.swapaxes(w_uv, 1, 2).astype(BF16)

    n_qb = seq_len // Q_BLOCK
    seq_spec = lambda r, c: pl.BlockSpec((None, r, c), lambda b, i: (b, 0, 0))
    o = pl.pallas_call(
        functools.partial(_kernel_b, seq_len=seq_len, top_k=top_k),
        grid=(bsz, n_qb),
        in_specs=[seq_spec(seq_len, V7X_LANES), seq_spec(seq_len, KV_LORA),
                  pl.BlockSpec((None, n_kstep, PV_ROWS, KEY_STEP), lambda b, i: (b, 0, 0, 0)),
                  pl.BlockSpec((None, IDX_HEADS, Q_BLOCK), lambda b, i: (b, 0, i)),
                  pl.BlockSpec((Q_BLOCK, n_qidx), lambda b, i: (b * n_qb + i, 0)),
                  pl.BlockSpec((Q_BLOCK, n_lat), lambda b, i: (b * n_qb + i, 0)),
                  pl.BlockSpec(wuvt.shape, lambda b, i: (0, 0, 0))],
        out_specs=pl.BlockSpec((Q_BLOCK, N_HEADS * V_DIM), lambda b, i: (b * n_qb + i, 0)),
        out_shape=jax.ShapeDtypeStruct((n_tok, N_HEADS * V_DIM), BF16),
        scratch_shapes=[pltpu.VMEM((IDX_HEADS * Q_BLOCK, V7X_LANES), BF16),
                        pltpu.VMEM((N_HEADS * Q_BLOCK, KV_LORA + Q_BLOCK), BF16),
                        pltpu.VMEM((seq_len, Q_BLOCK), I32),
                        pltpu.VMEM((seq_len, N_HEADS * Q_BLOCK), F32),
                        pltpu.VMEM((1, N_HEADS * Q_BLOCK), F32),
                        pltpu.VMEM((PV_ROWS, N_HEADS * Q_BLOCK), F32),
                        pltpu.VMEM((N_HEADS * V_DIM, Q_BLOCK), F32)],
        compiler_params=_params(("arbitrary", "arbitrary")),
        name="sparse_attn",
    )(kidx, ckv, ckvt, widx_t, qidx, qlat, wuvt)

    c_in = [x2, o, ma, gb, w_attn_out.astype(BF16), w_o.astype(BF16), row(mlp_norm_g),
            w_ff1.astype(BF16), w_ff2.astype(BF16), row(out_norm_g)]
    c_specs = [tok_spec(tm_c, D_MODEL)] * 4 + [_const_spec(a.shape) for a in c_in[4:]]
    out = pl.pallas_call(
        functools.partial(_kernel_c, ff_step=D_MODEL),
        grid=(n_tok // tm_c,),
        in_specs=c_specs,
        out_specs=tok_spec(tm_c, D_MODEL),
        out_shape=jax.ShapeDtypeStruct((n_tok, D_MODEL), F32),
        compiler_params=_params(("arbitrary",)),
        name="block_out",
    )(*c_in)
    return out.reshape(bsz, seq_len, D_MODEL)


def kernel(x, attn_norm_g, w_in, b_gate, dw_w, dw_b, conv_ln_g, conv_ln_b, w_conv_out, q_norm_g,
           kv_norm_g, w_uq, w_uk, w_uv, w_qi, kidx_ln_g, kidx_ln_b, w_attn_out, w_o, mlp_norm_g,
           w_ff1, w_ff2, final_norm_g):
    assert attn_norm_g.shape[0] == 1, "one layer: the final RMSNorm is fused into its output kernel"
    return _layer(x, attn_norm_g[0], w_in[0], b_gate[0], dw_w[0], dw_b[0], conv_ln_g[0],
                  conv_ln_b[0], w_conv_out[0], q_norm_g[0], kv_norm_g[0], w_uq[0], w_uk[0], w_uv[0],
                  w_qi[0], kidx_ln_g[0], kidx_ln_b[0], w_attn_out[0], w_o[0], mlp_norm_g[0],
                  w_ff1[0], w_ff2[0], final_norm_g)
```

```python
import functools

import jax
import jax.numpy as jnp
from jax import lax
from jax.experimental import pallas as pl
from jax.experimental.pallas import tpu as pltpu

D_MODEL = 1024
CHUNK = 64
Q_BLOCK = 128
EPS = 1e-6
CONV_KERNEL = 31
N_HEADS = 16
HEAD_DIM = 64
V_DIM = 64
Q_LORA = 256
KV_LORA = 128
IDX_HEADS = 8
IDX_DIM = 64
IDX_TOPK_MAX = 256
D_FF = 4 * D_MODEL

V7X_LANES = 128
V7X_SUBLANES = 8
V7X_VMEM_LIMIT_BYTES = 60000 * 1024
MXU_COLS = 256

CONV_HALO = 32
CONV_ROWS = 64
NORM_ROWS = 64
KEY_STEP = 4 * Q_BLOCK
HALF_STEP = KEY_STEP // 2
SCORE_ROWS = Q_BLOCK
NEG_BIG = -1e30
INT_MIN = -(2 ** 31)
LOG2_E = 1.4426950408889634
BISECT_GROUP = 4
BISECT_UNCHECKED = 16
V7X_BF16_ROWS = 16
PV_ROWS = KV_LORA + V7X_BF16_ROWS

F32 = jnp.float32
BF16 = jnp.bfloat16
I32 = jnp.int32


def _dot(a, b):
    return jnp.dot(a, b, preferred_element_type=F32)


def _dot_nt(a, b):
    return lax.dot_general(a, b, (((1,), (1,)), ((), ())), preferred_element_type=F32)


def _rms(x, g):
    ms = jnp.mean(x * x, axis=-1, keepdims=True)
    return x * lax.rsqrt(ms + EPS) * g


def _sigmoid(x):
    return 1.0 / (1.0 + jnp.exp(-x))


def _kernel_a(x_ref, ng_ref, wbig_ref, wsm_ref, bgate_ref, dww_ref, dwb_ref, lng_ref, lnb_ref,
              wpw_ref, qng_ref, kvng_ref, wuq_ref, wukbd_ref, wqi_ref, klng_ref, klnb_ref,
              ma_ref, gb_ref, qlat_ref, qidx_ref, ckv_ref, ckvt_ref, kidx_ref, widx_ref,
              vbuf, ybuf, zbuf, gabuf, *, tm, tiles_per_seq):
    t = pl.program_id(0)
    first = (t % tiles_per_seq) == 0

    x = x_ref[...]
    h = _rms(x, ng_ref[...]).astype(BF16)

    n_lt = D_MODEL // V7X_LANES

    @pl.when(first)
    def _():
        vbuf[:, :, 0:CONV_HALO, :] = jnp.zeros((V7X_SUBLANES, n_lt, CONV_HALO, V7X_LANES), F32)

    @pl.when(jnp.logical_not(first))
    def _():
        vbuf[:, :, 0:CONV_HALO, :] = vbuf[:, :, tm:tm + CONV_HALO, :]

    for cb in range(D_MODEL // MXU_COLS):
        c0 = cb * MXU_COLS
        ua = _dot(h, wbig_ref[:, c0:c0 + MXU_COLS])
        ug = _dot(h, wbig_ref[:, D_MODEL + c0:D_MODEL + c0 + MXU_COLS])
        v = ua * _sigmoid(ug)
        for s in range(V7X_SUBLANES):
            for l2 in range(MXU_COLS // V7X_LANES):
                lt = c0 // V7X_LANES + l2
                vbuf[s, lt, CONV_HALO - s:CONV_HALO - s + tm, :] = v[:, l2 * V7X_LANES:(l2 + 1) * V7X_LANES]

    tap0 = CONV_HALO - (CONV_KERNEL - 1)

    row_groups = CONV_ROWS // V7X_SUBLANES

    def conv_chunk(r, carry):
        base = r * CONV_ROWS
        for lt in range(n_lt):
            lanes = slice(lt * V7X_LANES, (lt + 1) * V7X_LANES)
            acc = jnp.zeros((row_groups, V7X_SUBLANES, V7X_LANES), F32)
            for k in range(CONV_KERNEL):
                j, s = divmod(tap0 + k, V7X_SUBLANES)
                row0 = base + V7X_SUBLANES * j
                win = vbuf[s, lt, pl.ds(row0, CONV_ROWS), :]
                acc = acc + dww_ref[k, :, lanes][None] * win.reshape(row_groups, V7X_SUBLANES, V7X_LANES)
            ybuf[pl.ds(base, CONV_ROWS), lanes] = acc.reshape(CONV_ROWS, V7X_LANES)
        return carry

    assert tm // CONV_ROWS == 4 and NORM_ROWS == CONV_ROWS
    def norm_chunk(r, carry):
        base = r * NORM_ROWS
        y = ybuf[pl.ds(base, NORM_ROWS), :] + dwb_ref[...]
        mu = jnp.mean(y, axis=-1, keepdims=True)
        yc = y - mu
        var = jnp.mean(yc * yc, axis=-1, keepdims=True)
        z = yc * lax.rsqrt(var + EPS) * lng_ref[...] + lnb_ref[...]
        z = z * _sigmoid(z)
        zbuf[pl.ds(base, NORM_ROWS), :] = z.astype(BF16)
        return carry

    conv_chunk(0, 0)
    for cb in range(D_MODEL // MXU_COLS):
        c0 = cb * MXU_COLS
        gabuf[:, c0:c0 + MXU_COLS] = _sigmoid(
            _dot(h, wbig_ref[:, 2 * D_MODEL + c0:2 * D_MODEL + c0 + MXU_COLS]) + bgate_ref[:, c0:c0 + MXU_COLS])
    conv_chunk(1, 0)
    norm_chunk(0, 0)
    for cb in range(D_MODEL // MXU_COLS):
        c0 = cb * MXU_COLS
        gb = _sigmoid(_dot(h, wbig_ref[:, 3 * D_MODEL + c0:3 * D_MODEL + c0 + MXU_COLS])
                      + bgate_ref[:, D_MODEL + c0:D_MODEL + c0 + MXU_COLS])
        gb_ref[:, c0:c0 + MXU_COLS] = gb.astype(BF16)

    conv_chunk(2, 0)
    norm_chunk(1, 0)
    us = _dot(h, wsm_ref[...])
    cq = _rms(us[:, 0:Q_LORA], qng_ref[...]).astype(BF16)
    ckv = _rms(us[:, Q_LORA:Q_LORA + KV_LORA], kvng_ref[...])
    ckv_ref[...] = ckv.astype(BF16)
    ckvt_ref[0:KV_LORA, :] = ckv.T.astype(BF16)
    ckvt_ref[KV_LORA:PV_ROWS, :] = jnp.where(
        lax.broadcasted_iota(I32, (PV_ROWS - KV_LORA, tm), 0) == 0, 1.0, 0.0).astype(BF16)

    t3 = us[:, Q_LORA + KV_LORA:Q_LORA + KV_LORA + V7X_LANES]
    lane = lax.broadcasted_iota(I32, t3.shape, 1)
    is_k = lane < IDX_DIM
    mu = jnp.sum(jnp.where(is_k, t3, 0.0), axis=-1, keepdims=True) * (1.0 / IDX_DIM)
    kc = jnp.where(is_k, t3 - mu, 0.0)
    var = jnp.sum(kc * kc, axis=-1, keepdims=True) * (1.0 / IDX_DIM)
    kn = kc * lax.rsqrt(var + EPS) * klng_ref[...] + klnb_ref[...]
    kidx_ref[...] = (kn + pltpu.roll(kn, IDX_DIM, 1)).astype(BF16)
    wi = t3 * (IDX_HEADS ** -0.5 * IDX_DIM ** -0.5)
    widx_ref[...] = wi.T[IDX_DIM:IDX_DIM + IDX_HEADS, :]

    conv_chunk(3, 0)
    norm_chunk(2, 0)
    q = _dot(cq, wuq_ref[...]).astype(BF16)
    heads_per_group = wukbd_ref.shape[1] // HEAD_DIM
    lat_per_group = heads_per_group * KV_LORA
    for g in range(N_HEADS // heads_per_group):
        ql = _dot(q[:, g * heads_per_group * HEAD_DIM:(g + 1) * heads_per_group * HEAD_DIM],
                  wukbd_ref[g])
        qlat_ref[:, g * lat_per_group:(g + 1) * lat_per_group] = (ql * (HEAD_DIM ** -0.5 * LOG2_E)).astype(BF16)
    qidx_ref[...] = _dot(cq, wqi_ref[...]).astype(BF16)

    norm_chunk(3, 0)
    ya = _dot(zbuf[...], wpw_ref[...])
    ma_ref[...] = (gabuf[...] * ya).astype(BF16)


def _kernel_b(kidx_ref, ckv_ref, ckvt_ref, widx_ref, qidx_ref, qlat_ref, wuvt_ref, o_ref,
              qall_scr, qlat_scr, key_scr, s_scr, m_scr, acc_scr, ot_scr,
              *, seq_len, top_k):
    i = pl.program_id(1)
    n_q = Q_BLOCK
    blk_per_step = KEY_STEP // Q_BLOCK
    rem = (i + 1) % blk_per_step
    has_half = jnp.logical_and(rem > 0, rem <= HALF_STEP // Q_BLOCK)
    n_full = (i + 1) // blk_per_step + jnp.where(rem > HALF_STEP // Q_BLOCK, 1, 0)
    tail0 = pl.multiple_of(n_full * KEY_STEP, KEY_STEP)

    def over_keys(step_fn, init):
        def full(c, carry):
            return step_fn(pl.multiple_of(c * KEY_STEP, KEY_STEP), KEY_STEP, c, carry)
        carry = lax.fori_loop(0, n_full, full, init)
        return lax.cond(has_half, lambda cr: step_fn(tail0, HALF_STEP, n_full, cr), lambda cr: cr, carry)

    lane = lax.broadcasted_iota(I32, (n_q, V7X_LANES), 1)
    for hh in range(IDX_HEADS):
        tile = qidx_ref[:, (hh // 2) * V7X_LANES:(hh // 2 + 1) * V7X_LANES].astype(F32)
        keep = (lane < IDX_DIM) if hh % 2 == 0 else (lane >= IDX_DIM)
        qall_scr[hh * n_q:(hh + 1) * n_q, :] = jnp.where(keep, tile, 0.0).astype(BF16)
    for hh in range(N_HEADS):
        qlat_scr[hh * n_q:(hh + 1) * n_q, 0:KV_LORA] = qlat_ref[:, hh * KV_LORA:(hh + 1) * KV_LORA]

    @pl.when(i == 0)
    def _():
        eye = (lax.broadcasted_iota(I32, (n_q, n_q), 0) == lax.broadcasted_iota(I32, (n_q, n_q), 1))
        eye = jnp.where(eye, 1.0, 0.0).astype(BF16)
        for hh in range(N_HEADS):
            qlat_scr[hh * n_q:(hh + 1) * n_q, KV_LORA:KV_LORA + n_q] = eye

    qabs_chunk = (i * n_q + lax.broadcasted_iota(I32, (SCORE_ROWS, n_q), 1)) // CHUNK

    def score_step(first, rows, c, carry):
        for sub in range(rows // SCORE_ROWS):
            r0 = pl.multiple_of(first + sub * SCORE_ROWS, SCORE_ROWS)
            kx = kidx_ref[pl.ds(r0, SCORE_ROWS), :]
            sc = jnp.zeros((SCORE_ROWS, n_q), F32)
            for hp in range(IDX_HEADS // 2):
                lg = _dot_nt(kx, qall_scr[hp * MXU_COLS:(hp + 1) * MXU_COLS, :])
                for h2 in range(2):
                    hh = 2 * hp + h2
                    sc = sc + widx_ref[hh:hh + 1, :] * jnp.maximum(lg[:, h2 * n_q:(h2 + 1) * n_q], 0.0)
            bits = pltpu.bitcast(sc, I32)
            key = bits ^ ((bits >> 31) & 0x7FFFFFFF)
            key = jnp.where(bits == INT_MIN, 0, key)
            kabs_chunk = (r0 + lax.broadcasted_iota(I32, (SCORE_ROWS, n_q), 0)) // CHUNK
            key_scr[pl.ds(r0, SCORE_ROWS), :] = jnp.where(kabs_chunk <= qabs_chunk, key, INT_MIN)
        return carry

    over_keys(score_step, 0)

    def count_rows(pred):
        def body(r0, rows, c, cnt):
            hit = pred(key_scr[pl.ds(r0, rows), :], r0)
            ones = jnp.where(hit, 1, 0).astype(I32)
            return cnt + jnp.sum(ones.reshape(rows // V7X_SUBLANES, V7X_SUBLANES, n_q), axis=0)
        cnt8 = over_keys(body, jnp.zeros((V7X_SUBLANES, n_q), I32))
        return jnp.sum(cnt8, axis=0, keepdims=True)

    def bisect_bits(it, cur, settled, n_bits):
        for j in range(n_bits):
            cand = cur ^ jnp.left_shift(jnp.int32(1), 31 - (it + j))
            cnt = count_rows(lambda k, r0: k >= cand)
            step = jnp.where(cnt >= top_k, cand, cur)
            cur = jnp.where(settled > 0, cur, step)
            settled = jnp.maximum(settled, jnp.where(cnt == top_k, 1, 0))
        return cur, settled

    def open_count(settled):
        return n_q - jnp.sum(settled.astype(F32))

    def unchecked_group(g, state):
        return bisect_bits(g * BISECT_GROUP, *state, BISECT_GROUP)

    cur0, settled0 = lax.fori_loop(
        0, BISECT_UNCHECKED // BISECT_GROUP, unchecked_group,
        (jnp.full((1, n_q), INT_MIN, I32), jnp.zeros((1, n_q), I32)))

    def bisect_group(state):
        it, cur, settled, _ = state
        cur, settled = bisect_bits(it, cur, settled, BISECT_GROUP)
        return it + BISECT_GROUP, cur, settled, open_count(settled)

    def bisect_more(state):
        it, _, _, open_lanes = state
        return jnp.logical_and(it < 32, open_lanes > 0.0)

    _, thr, _, open_lanes = lax.while_loop(
        bisect_more, bisect_group, (jnp.int32(BISECT_UNCHECKED), cur0, settled0, open_count(settled0)))
    thr = jnp.maximum(thr, INT_MIN + 1)
    all_rows = lambda: jnp.full((1, n_q), seq_len, I32)

    def tie_rows():
        n_gt = count_rows(lambda k, r0: k > thr)
        n_ge = count_rows(lambda k, r0: k >= thr)
        need = top_k - n_gt

        def tie_limit():
            n_bits = max(1, (seq_len - 1).bit_length())

            def bisect_row(it, cur):
                cand = cur + jnp.left_shift(jnp.int32(1), n_bits - 1 - it)

                def pred(k, r0):
                    row = r0 + lax.broadcasted_iota(I32, k.shape, 0)
                    return jnp.logical_and(k == thr, row < cand)
                cnt = count_rows(pred)
                return jnp.where(cnt < need, cand, cur)

            return lax.fori_loop(0, n_bits, bisect_row, jnp.zeros((1, n_q), I32))

        return lax.cond(jnp.max(n_ge.astype(F32)) > top_k, tie_limit, all_rows)

    row_lim = lax.cond(open_lanes > 0.0, tie_rows, all_rows)

    pair = 2 * n_q
    n_pair = N_HEADS // 2
    n_col = N_HEADS * n_q

    def qk_step(r0, rows, c, m8):
        k = key_scr[pl.ds(r0, rows), :]
        row = r0 + lax.broadcasted_iota(I32, (rows, n_q), 0)
        take = jnp.logical_or(k > thr, jnp.logical_and(k == thr, row <= row_lim))
        mask = jnp.where(take, 0.0, NEG_BIG).astype(BF16)
        kvb = jnp.concatenate([ckv_ref[pl.ds(r0, rows), :], mask], axis=1)
        parts = []
        for hp in range(n_pair):
            s = _dot_nt(kvb, qlat_scr[hp * pair:(hp + 1) * pair, :])
            s_scr[pl.ds(r0, rows), hp * pair:(hp + 1) * pair] = s
            parts.append(jnp.max(s.reshape(rows // V7X_SUBLANES, V7X_SUBLANES, pair), axis=0))
        return jnp.maximum(m8, jnp.concatenate(parts, axis=1))

    m8 = over_keys(qk_step, jnp.full((V7X_SUBLANES, n_col), NEG_BIG, F32))
    m_scr[...] = jnp.max(m8, axis=0, keepdims=True)

    acc_scr[...] = jnp.zeros(acc_scr.shape, F32)

    def pv_step(r0, rows, c, carry):
        kvt = ckvt_ref[c, :, 0:rows]
        for hp in range(n_pair):
            cols = slice(hp * pair, (hp + 1) * pair)
            p = jnp.exp2(s_scr[pl.ds(r0, rows), cols] - m_scr[:, cols])
            acc_scr[:, cols] += _dot(kvt, p.astype(BF16))
        return carry

    over_keys(pv_step, 0)

    for hh in range(N_HEADS):
        cols = slice(hh * n_q, (hh + 1) * n_q)
        inv_l = 1.0 / acc_scr[KV_LORA:KV_LORA + 1, cols]
        olat = (acc_scr[0:KV_LORA, cols] * inv_l).astype(BF16)
        ot_scr[hh * V_DIM:(hh + 1) * V_DIM, :] = _dot(wuvt_ref[hh], olat)
    o_ref[...] = ot_scr[...].T.astype(BF16)


def _kernel_c(x_ref, o_ref, ma_ref, gb_ref, wao_ref, wo_ref, mng_ref, w1_ref, w2_ref, fng_ref,
              out_ref, *, ff_step):
    yb = _dot(o_ref[...], wao_ref[...])
    merged = ma_ref[...].astype(F32) + gb_ref[...].astype(F32) * yb
    x1 = x_ref[...] + _dot(merged.astype(BF16), wo_ref[...])
    hm = _rms(x1, mng_ref[...]).astype(BF16)
    acc = x1
    for j in range(D_FF // ff_step):
        h1 = jnp.maximum(_dot(hm, w1_ref[:, j * ff_step:(j + 1) * ff_step]), 0.0)
        acc = acc + _dot((h1 * h1).astype(BF16), w2_ref[j * ff_step:(j + 1) * ff_step, :])
    out_ref[...] = _rms(acc, fng_ref[...])


def _const_spec(shape):
    nd = len(shape)
    return pl.BlockSpec(shape, lambda *_: (0,) * nd, pipeline_mode=pl.Buffered(1))


def _params(semantics):
    return pltpu.CompilerParams(dimension_semantics=semantics,
                                vmem_limit_bytes=V7X_VMEM_LIMIT_BYTES)


def _layer(x, attn_norm_g, w_in, b_gate, dw_w, dw_b, conv_ln_g, conv_ln_b, w_conv_out, q_norm_g,
           kv_norm_g, w_uq, w_uk, w_uv, w_qi, kidx_ln_g, kidx_ln_b, w_attn_out, w_o, mlp_norm_g,
           w_ff1, w_ff2, out_norm_g):
    bsz, seq_len, _ = x.shape
    n_tok = bsz * seq_len
    top_k = min(IDX_TOPK_MAX, seq_len // 4)
    tm_a = min(256, seq_len)
    tm_c = min(512, seq_len)
    assert seq_len % Q_BLOCK == 0 and seq_len % tm_a == 0 and seq_len % tm_c == 0
    assert seq_len % KEY_STEP == 0

    row = lambda v: v.reshape(1, -1).astype(F32)
    x2 = x.reshape(n_tok, D_MODEL)

    o_cq = 2 * D_MODEL
    o_ckv = o_cq + Q_LORA
    o_kidx = o_ckv + KV_LORA
    o_widx = o_kidx + IDX_DIM
    o_gate = o_widx + IDX_HEADS
    wbig = jnp.concatenate([w_in[:, 0:2 * D_MODEL], w_in[:, o_gate:o_gate + 2 * D_MODEL]],
                           axis=1).astype(BF16)
    n_small = o_gate - o_cq
    wsm = jnp.pad(w_in[:, o_cq:o_gate], ((0, 0), (0, 4 * V7X_LANES - n_small))).astype(BF16)

    dww = jnp.broadcast_to(dw_w.astype(F32)[:, None, :], (CONV_KERNEL, V7X_SUBLANES, D_MODEL))
    klng = jnp.pad(kidx_ln_g, (0, V7X_LANES - IDX_DIM)).reshape(1, -1).astype(F32)
    klnb = jnp.pad(kidx_ln_b, (0, V7X_LANES - IDX_DIM)).reshape(1, -1).astype(F32)

    hpg = 4
    wuk_g = w_uk.reshape(N_HEADS // hpg, hpg, HEAD_DIM, KV_LORA)
    eye = jnp.eye(hpg, dtype=w_uk.dtype)
    wukbd = jnp.einsum("ghdc,hk->ghdkc", wuk_g, eye).reshape(
        N_HEADS // hpg, hpg * HEAD_DIM, hpg * KV_LORA).astype(BF16)

    tiles_per_seq = seq_len // tm_a
    tok_spec = lambda tm, n: pl.BlockSpec((tm, n), lambda t: (t, 0))
    a_in = [x2, row(attn_norm_g), wbig, wsm, row(b_gate), dww, row(dw_b), row(conv_ln_g),
            row(conv_ln_b), w_conv_out.astype(BF16), row(q_norm_g), row(kv_norm_g),
            w_uq.astype(BF16), wukbd, w_qi.astype(BF16), klng, klnb]
    a_specs = [tok_spec(tm_a, D_MODEL)] + [_const_spec(a.shape) for a in a_in[1:]]
    n_lat = N_HEADS * KV_LORA
    n_qidx = IDX_HEADS * IDX_DIM
    n_kstep = seq_len // KEY_STEP
    tiles_per_kstep = KEY_STEP // tm_a
    ma, gb, qlat, qidx, ckv, ckvt, kidx, widx_t = pl.pallas_call(
        functools.partial(_kernel_a, tm=tm_a, tiles_per_seq=tiles_per_seq),
        grid=(n_tok // tm_a,),
        in_specs=a_specs,
        out_specs=[tok_spec(tm_a, D_MODEL), tok_spec(tm_a, D_MODEL), tok_spec(tm_a, n_lat),
                   tok_spec(tm_a, n_qidx), tok_spec(tm_a, KV_LORA),
                   pl.BlockSpec((None, None, PV_ROWS, tm_a),
                                lambda t: (t // tiles_per_seq, (t % tiles_per_seq) // tiles_per_kstep, 0,
                                           t % tiles_per_kstep)),
                   tok_spec(tm_a, V7X_LANES),
                   pl.BlockSpec((None, IDX_HEADS, tm_a), lambda t: (t // tiles_per_seq, 0, t % tiles_per_seq))],
        out_shape=[jax.ShapeDtypeStruct((n_tok, D_MODEL), BF16),
                   jax.ShapeDtypeStruct((n_tok, D_MODEL), BF16),
                   jax.ShapeDtypeStruct((n_tok, n_lat), BF16),
                   jax.ShapeDtypeStruct((n_tok, n_qidx), BF16),
                   jax.ShapeDtypeStruct((n_tok, KV_LORA), BF16),
                   jax.ShapeDtypeStruct((bsz, n_kstep, PV_ROWS, KEY_STEP), BF16),
                   jax.ShapeDtypeStruct((n_tok, V7X_LANES), BF16),
                   jax.ShapeDtypeStruct((bsz, IDX_HEADS, seq_len), F32)],
        scratch_shapes=[pltpu.VMEM((V7X_SUBLANES, D_MODEL // V7X_LANES, CONV_HALO + tm_a, V7X_LANES), F32),
                        pltpu.VMEM((tm_a, D_MODEL), F32),
                        pltpu.VMEM((tm_a, D_MODEL), BF16),
                        pltpu.VMEM((tm_a, D_MODEL), F32)],
        compiler_params=_params(("arbitrary",)),
        name="block_in",
    )(*a_in)

    ckv = ckv.reshape(bsz, seq_len, KV_LORA)
    kidx = kidx.reshape(bsz, seq_len, V7X_LANES)
    wuvt = jnp.swapaxes(w_uv, 1, 2).astype(BF16)

    n_qb = seq_len // Q_BLOCK
    seq_spec = lambda r, c: pl.BlockSpec((None, r, c), lambda b, i: (b, 0, 0))
    o = pl.pallas_call(
        functools.partial(_kernel_b, seq_len=seq_len, top_k=top_k),
        grid=(bsz, n_qb),
        in_specs=[seq_spec(seq_len, V7X_LANES), seq_spec(seq_len, KV_LORA),
                  pl.BlockSpec((None, n_kstep, PV_ROWS, KEY_STEP), lambda b, i: (b, 0, 0, 0)),
                  pl.BlockSpec((None, IDX_HEADS, Q_BLOCK), lambda b, i: (b, 0, i)),
                  pl.BlockSpec((Q_BLOCK, n_qidx), lambda b, i: (b * n_qb + i, 0)),
                  pl.BlockSpec((Q_BLOCK, n_lat), lambda b, i: (b * n_qb + i, 0)),
                  pl.BlockSpec(wuvt.shape, lambda b, i: (0, 0, 0))],
        out_specs=pl.BlockSpec((Q_BLOCK, N_HEADS * V_DIM), lambda b, i: (b * n_qb + i, 0)),
        out_shape=jax.ShapeDtypeStruct((n_tok, N_HEADS * V_DIM), BF16),
        scratch_shapes=[pltpu.VMEM((IDX_HEADS * Q_BLOCK, V7X_LANES), BF16),
                        pltpu.VMEM((N_HEADS * Q_BLOCK, KV_LORA + Q_BLOCK), BF16),
                        pltpu.VMEM((seq_len, Q_BLOCK), I32),
                        pltpu.VMEM((seq_len, N_HEADS * Q_BLOCK), F32),
                        pltpu.VMEM((1, N_HEADS * Q_BLOCK), F32),
                        pltpu.VMEM((PV_ROWS, N_HEADS * Q_BLOCK), F32),
                        pltpu.VMEM((N_HEADS * V_DIM, Q_BLOCK), F32)],
        compiler_params=_params(("arbitrary", "arbitrary")),
        name="sparse_attn",
    )(kidx, ckv, ckvt, widx_t, qidx, qlat, wuvt)

    c_in = [x2, o, ma, gb, w_attn_out.astype(BF16), w_o.astype(BF16), row(mlp_norm_g),
            w_ff1.astype(BF16), w_ff2.astype(BF16), row(out_norm_g)]
    c_specs = [tok_spec(tm_c, D_MODEL)] * 4 + [_const_spec(a.shape) for a in c_in[4:]]
    out = pl.pallas_call(
        functools.partial(_kernel_c, ff_step=D_MODEL),
        grid=(n_tok // tm_c,),
        in_specs=c_specs,
        out_specs=tok_spec(tm_c, D_MODEL),
        out_shape=jax.ShapeDtypeStruct((n_tok, D_MODEL), F32),
        compiler_params=_params(("arbitrary",)),
        name="block_out",
    )(*c_in)
    return out.reshape(bsz, seq_len, D_MODEL)


def kernel(x, attn_norm_g, w_in, b_gate, dw_w, dw_b, conv_ln_g, conv_ln_b, w_conv_out, q_norm_g,
           kv_norm_g, w_uq, w_uk, w_uv, w_qi, kidx_ln_g, kidx_ln_b, w_attn_out, w_o, mlp_norm_g,
           w_ff1, w_ff2, final_norm_g):
    assert attn_norm_g.shape[0] == 1, "one layer: the final RMSNorm is fused into its output kernel"
    return _layer(x, attn_norm_g[0], w_in[0], b_gate[0], dw_w[0], dw_b[0], conv_ln_g[0],
                  conv_ln_b[0], w_conv_out[0], q_norm_g[0], kv_norm_g[0], w_uq[0], w_uk[0], w_uv[0],
                  w_qi[0], kidx_ln_g[0], kidx_ln_b[0], w_attn_out[0], w_o[0], mlp_norm_g[0],
                  w_ff1[0], w_ff2[0], final_norm_g)
```

```python
import functools

import jax
import jax.numpy as jnp
from jax import lax
from jax.experimental import pallas as pl
from jax.experimental.pallas import tpu as pltpu

D_MODEL = 1024
CHUNK = 64
Q_BLOCK = 128
EPS = 1e-6
CONV_KERNEL = 31
N_HEADS = 16
HEAD_DIM = 64
V_DIM = 64
Q_LORA = 256
KV_LORA = 128
IDX_HEADS = 8
IDX_DIM = 64
IDX_TOPK_MAX = 256
D_FF = 4 * D_MODEL

V7X_LANES = 128
V7X_SUBLANES = 8
V7X_VMEM_LIMIT_BYTES = 60000 * 1024
MXU_COLS = 256

CONV_HALO = 32
CONV_ROWS = 64
NORM_ROWS = 64
KEY_STEP = 4 * Q_BLOCK
HALF_STEP = KEY_STEP // 2
SCORE_ROWS = Q_BLOCK
NEG_BIG = -1e30
INT_MIN = -(2 ** 31)
LOG2_E = 1.4426950408889634
BISECT_GROUP = 4
BISECT_UNCHECKED = 20
V7X_BF16_ROWS = 16
PV_ROWS = KV_LORA + V7X_BF16_ROWS

F32 = jnp.float32
BF16 = jnp.bfloat16
I32 = jnp.int32


def _dot(a, b):
    return jnp.dot(a, b, preferred_element_type=F32)


def _dot_nt(a, b):
    return lax.dot_general(a, b, (((1,), (1,)), ((), ())), preferred_element_type=F32)


def _rms(x, g):
    ms = jnp.mean(x * x, axis=-1, keepdims=True)
    return x * lax.rsqrt(ms + EPS) * g


def _sigmoid(x):
    return 1.0 / (1.0 + jnp.exp(-x))


def _kernel_a(x_ref, ng_ref, wbig_ref, wsm_ref, bgate_ref, dww_ref, dwb_ref, lng_ref, lnb_ref,
              wpw_ref, qng_ref, kvng_ref, wuq_ref, wukbd_ref, wqi_ref, klng_ref, klnb_ref,
              ma_ref, gb_ref, qlat_ref, qidx_ref, ckv_ref, ckvt_ref, kidx_ref, widx_ref,
              vbuf, ybuf, zbuf, gabuf, *, tm, tiles_per_seq):
    t = pl.program_id(0)
    first = (t % tiles_per_seq) == 0

    x = x_ref[...]
    h = _rms(x, ng_ref[...]).astype(BF16)

    n_lt = D_MODEL // V7X_LANES

    @pl.when(first)
    def _():
        vbuf[:, :, 0:CONV_HALO, :] = jnp.zeros((V7X_SUBLANES, n_lt, CONV_HALO, V7X_LANES), F32)

    @pl.when(jnp.logical_not(first))
    def _():
        vbuf[:, :, 0:CONV_HALO, :] = vbuf[:, :, tm:tm + CONV_HALO, :]

    for cb in range(D_MODEL // MXU_COLS):
        c0 = cb * MXU_COLS
        ua = _dot(h, wbig_ref[:, c0:c0 + MXU_COLS])
        ug = _dot(h, wbig_ref[:, D_MODEL + c0:D_MODEL + c0 + MXU_COLS])
        v = ua * _sigmoid(ug)
        for s in range(V7X_SUBLANES):
            for l2 in range(MXU_COLS // V7X_LANES):
                lt = c0 // V7X_LANES + l2
                vbuf[s, lt, CONV_HALO - s:CONV_HALO - s + tm, :] = v[:, l2 * V7X_LANES:(l2 + 1) * V7X_LANES]

    tap0 = CONV_HALO - (CONV_KERNEL - 1)

    row_groups = CONV_ROWS // V7X_SUBLANES

    def conv_chunk(r, carry):
        base = r * CONV_ROWS
        for lt in range(n_lt):
            lanes = slice(lt * V7X_LANES, (lt + 1) * V7X_LANES)
            acc = jnp.zeros((row_groups, V7X_SUBLANES, V7X_LANES), F32)
            for k in range(CONV_KERNEL):
                j, s = divmod(tap0 + k, V7X_SUBLANES)
                row0 = base + V7X_SUBLANES * j
                win = vbuf[s, lt, pl.ds(row0, CONV_ROWS), :]
                acc = acc + dww_ref[k, :, lanes][None] * win.reshape(row_groups, V7X_SUBLANES, V7X_LANES)
            ybuf[pl.ds(base, CONV_ROWS), lanes] = acc.reshape(CONV_ROWS, V7X_LANES)
        return carry

    assert tm // CONV_ROWS == 4 and NORM_ROWS == CONV_ROWS
    def norm_chunk(r, carry):
        base = r * NORM_ROWS
        y = ybuf[pl.ds(base, NORM_ROWS), :] + dwb_ref[...]
        mu = jnp.mean(y, axis=-1, keepdims=True)
        yc = y - mu
        var = jnp.mean(yc * yc, axis=-1, keepdims=True)
        z = yc * lax.rsqrt(var + EPS) * lng_ref[...] + lnb_ref[...]
        z = z * _sigmoid(z)
        zbuf[pl.ds(base, NORM_ROWS), :] = z.astype(BF16)
        return carry

    conv_chunk(0, 0)
    for cb in range(D_MODEL // MXU_COLS):
        c0 = cb * MXU_COLS
        gabuf[:, c0:c0 + MXU_COLS] = _sigmoid(
            _dot(h, wbig_ref[:, 2 * D_MODEL + c0:2 * D_MODEL + c0 + MXU_COLS]) + bgate_ref[:, c0:c0 + MXU_COLS])
    conv_chunk(1, 0)
    norm_chunk(0, 0)
    for cb in range(D_MODEL // MXU_COLS):
        c0 = cb * MXU_COLS
        gb = _sigmoid(_dot(h, wbig_ref[:, 3 * D_MODEL + c0:3 * D_MODEL + c0 + MXU_COLS])
                      + bgate_ref[:, D_MODEL + c0:D_MODEL + c0 + MXU_COLS])
        gb_ref[:, c0:c0 + MXU_COLS] = gb.astype(BF16)

    conv_chunk(2, 0)
    norm_chunk(1, 0)
    us = _dot(h, wsm_ref[...])
    cq = _rms(us[:, 0:Q_LORA], qng_ref[...]).astype(BF16)
    ckv = _rms(us[:, Q_LORA:Q_LORA + KV_LORA], kvng_ref[...])
    ckv_ref[...] = ckv.astype(BF16)
    ckvt_ref[0:KV_LORA, :] = ckv.T.astype(BF16)
    ckvt_ref[KV_LORA:PV_ROWS, :] = jnp.where(
        lax.broadcasted_iota(I32, (PV_ROWS - KV_LORA, tm), 0) == 0, 1.0, 0.0).astype(BF16)

    t3 = us[:, Q_LORA + KV_LORA:Q_LORA + KV_LORA + V7X_LANES]
    lane = lax.broadcasted_iota(I32, t3.shape, 1)
    is_k = lane < IDX_DIM
    mu = jnp.sum(jnp.where(is_k, t3, 0.0), axis=-1, keepdims=True) * (1.0 / IDX_DIM)
    kc = jnp.where(is_k, t3 - mu, 0.0)
    var = jnp.sum(kc * kc, axis=-1, keepdims=True) * (1.0 / IDX_DIM)
    kn = kc * lax.rsqrt(var + EPS) * klng_ref[...] + klnb_ref[...]
    kidx_ref[...] = (kn + pltpu.roll(kn, IDX_DIM, 1)).astype(BF16)
    wi = t3 * (IDX_HEADS ** -0.5 * IDX_DIM ** -0.5)
    widx_ref[...] = wi.T[IDX_DIM:IDX_DIM + IDX_HEADS, :]

    conv_chunk(3, 0)
    norm_chunk(2, 0)
    q = _dot(cq, wuq_ref[...]).astype(BF16)
    heads_per_group = wukbd_ref.shape[1] // HEAD_DIM
    lat_per_group = heads_per_group * KV_LORA
    for g in range(N_HEADS // heads_per_group):
        ql = _dot(q[:, g * heads_per_group * HEAD_DIM:(g + 1) * heads_per_group * HEAD_DIM],
                  wukbd_ref[g])
        qlat_ref[:, g * lat_per_group:(g + 1) * lat_per_group] = (ql * (HEAD_DIM ** -0.5 * LOG2_E)).astype(BF16)
    qidx_ref[...] = _dot(cq, wqi_ref[...]).astype(BF16)

    norm_chunk(3, 0)
    ya = _dot(zbuf[...], wpw_ref[...])
    ma_ref[...] = (gabuf[...] * ya).astype(BF16)


def _kernel_b(kidx_ref, ckv_ref, ckvt_ref, widx_ref, qidx_ref, qlat_ref, wuvt_ref, o_ref,
              qall_scr, qlat_scr, key_scr, s_scr, m_scr, acc_scr, ot_scr,
              *, seq_len, top_k):
    i = pl.program_id(1)
    n_q = Q_BLOCK
    blk_per_step = KEY_STEP // Q_BLOCK
    rem = (i + 1) % blk_per_step
    has_half = jnp.logical_and(rem > 0, rem <= HALF_STEP // Q_BLOCK)
    n_full = (i + 1) // blk_per_step + jnp.where(rem > HALF_STEP // Q_BLOCK, 1, 0)
    tail0 = pl.multiple_of(n_full * KEY_STEP, KEY_STEP)

    def over_keys(step_fn, init):
        def full(c, carry):
            return step_fn(pl.multiple_of(c * KEY_STEP, KEY_STEP), KEY_STEP, c, carry)
        carry = lax.fori_loop(0, n_full, full, init)
        return lax.cond(has_half, lambda cr: step_fn(tail0, HALF_STEP, n_full, cr), lambda cr: cr, carry)

    lane = lax.broadcasted_iota(I32, (n_q, V7X_LANES), 1)
    for hh in range(IDX_HEADS):
        tile = qidx_ref[:, (hh // 2) * V7X_LANES:(hh // 2 + 1) * V7X_LANES].astype(F32)
        keep = (lane < IDX_DIM) if hh % 2 == 0 else (lane >= IDX_DIM)
        qall_scr[hh * n_q:(hh + 1) * n_q, :] = jnp.where(keep, tile, 0.0).astype(BF16)
    for hh in range(N_HEADS):
        qlat_scr[hh * n_q:(hh + 1) * n_q, 0:KV_LORA] = qlat_ref[:, hh * KV_LORA:(hh + 1) * KV_LORA]

    @pl.when(i == 0)
    def _():
        eye = (lax.broadcasted_iota(I32, (n_q, n_q), 0) == lax.broadcasted_iota(I32, (n_q, n_q), 1))
        eye = jnp.where(eye, 1.0, 0.0).astype(BF16)
        for hh in range(N_HEADS):
            qlat_scr[hh * n_q:(hh + 1) * n_q, KV_LORA:KV_LORA + n_q] = eye

    qabs_chunk = (i * n_q + lax.broadcasted_iota(I32, (SCORE_ROWS, n_q), 1)) // CHUNK

    def score_step(first, rows, c, carry):
        for sub in range(rows // SCORE_ROWS):
            r0 = pl.multiple_of(first + sub * SCORE_ROWS, SCORE_ROWS)
            kx = kidx_ref[pl.ds(r0, SCORE_ROWS), :]
            sc = jnp.zeros((SCORE_ROWS, n_q), F32)
            for hp in range(IDX_HEADS // 2):
                lg = _dot_nt(kx, qall_scr[hp * MXU_COLS:(hp + 1) * MXU_COLS, :])
                for h2 in range(2):
                    hh = 2 * hp + h2
                    sc = sc + widx_ref[hh:hh + 1, :] * jnp.maximum(lg[:, h2 * n_q:(h2 + 1) * n_q], 0.0)
            bits = pltpu.bitcast(sc, I32)
            key = bits ^ ((bits >> 31) & 0x7FFFFFFF)
            key = jnp.where(bits == INT_MIN, 0, key)
            kabs_chunk = (r0 + lax.broadcasted_iota(I32, (SCORE_ROWS, n_q), 0)) // CHUNK
            key_scr[pl.ds(r0, SCORE_ROWS), :] = jnp.where(kabs_chunk <= qabs_chunk, key, INT_MIN)
        return carry

    over_keys(score_step, 0)

    def count_rows(pred):
        def body(r0, rows, c, cnt):
            hit = pred(key_scr[pl.ds(r0, rows), :], r0)
            ones = jnp.where(hit, 1, 0).astype(I32)
            return cnt + jnp.sum(ones.reshape(rows // V7X_SUBLANES, V7X_SUBLANES, n_q), axis=0)
        cnt8 = over_keys(body, jnp.zeros((V7X_SUBLANES, n_q), I32))
        return jnp.sum(cnt8, axis=0, keepdims=True)

    def bisect_bits(it, cur, settled, n_bits):
        for j in range(n_bits):
            cand = cur ^ jnp.left_shift(jnp.int32(1), 31 - (it + j))
            cnt = count_rows(lambda k, r0: k >= cand)
            step = jnp.where(cnt >= top_k, cand, cur)
            cur = jnp.where(settled > 0, cur, step)
            settled = jnp.maximum(settled, jnp.where(cnt == top_k, 1, 0))
        return cur, settled

    def open_count(settled):
        return n_q - jnp.sum(settled.astype(F32))

    def unchecked_group(g, state):
        return bisect_bits(g * BISECT_GROUP, *state, BISECT_GROUP)

    def bisect_group(state):
        it, cur, settled, _ = state
        cur, settled = bisect_bits(it, cur, settled, BISECT_GROUP)
        return it + BISECT_GROUP, cur, settled, open_count(settled)

    def bisect_more(state):
        it, _, _, open_lanes = state
        return jnp.logical_and(it < 32, open_lanes > 0.0)

    def search_threshold():
        cur0, settled0 = lax.fori_loop(
            0, BISECT_UNCHECKED // BISECT_GROUP, unchecked_group,
            (jnp.full((1, n_q), INT_MIN, I32), jnp.zeros((1, n_q), I32)))
        _, cur, _, still_open = lax.while_loop(
            bisect_more, bisect_group, (jnp.int32(BISECT_UNCHECKED), cur0, settled0, open_count(settled0)))
        return cur, still_open

    thr, open_lanes = lax.cond((i + 1) * Q_BLOCK > top_k, search_threshold,
                               lambda: (jnp.full((1, n_q), INT_MIN, I32), jnp.float32(0.0)))
    thr = jnp.maximum(thr, INT_MIN + 1)
    all_rows = lambda: jnp.full((1, n_q), seq_len, I32)

    def tie_rows():
        n_gt = count_rows(lambda k, r0: k > thr)
        n_ge = count_rows(lambda k, r0: k >= thr)
        need = top_k - n_gt

        def tie_limit():
            n_bits = max(1, (seq_len - 1).bit_length())

            def bisect_row(it, cur):
                cand = cur + jnp.left_shift(jnp.int32(1), n_bits - 1 - it)

                def pred(k, r0):
                    row = r0 + lax.broadcasted_iota(I32, k.shape, 0)
                    return jnp.logical_and(k == thr, row < cand)
                cnt = count_rows(pred)
                return jnp.where(cnt < need, cand, cur)

            return lax.fori_loop(0, n_bits, bisect_row, jnp.zeros((1, n_q), I32))

        return lax.cond(jnp.max(n_ge.astype(F32)) > top_k, tie_limit, all_rows)

    row_lim = lax.cond(open_lanes > 0.0, tie_rows, all_rows)

    pair = 2 * n_q
    n_pair = N_HEADS // 2
    n_col = N_HEADS * n_q

    def qk_step(r0, rows, c, m8):
        k = key_scr[pl.ds(r0, rows), :]
        row = r0 + lax.broadcasted_iota(I32, (rows, n_q), 0)
        take = jnp.logical_or(k > thr, jnp.logical_and(k == thr, row <= row_lim))
        mask = jnp.where(take, 0.0, NEG_BIG).astype(BF16)
        kvb = jnp.concatenate([ckv_ref[pl.ds(r0, rows), :], mask], axis=1)
        parts = []
        for hp in range(n_pair):
            s = _dot_nt(kvb, qlat_scr[hp * pair:(hp + 1) * pair, :])
            s_scr[pl.ds(r0, rows), hp * pair:(hp + 1) * pair] = s
            parts.append(jnp.max(s.reshape(rows // V7X_SUBLANES, V7X_SUBLANES, pair), axis=0))
        return jnp.maximum(m8, jnp.concatenate(parts, axis=1))

    m8 = over_keys(qk_step, jnp.full((V7X_SUBLANES, n_col), NEG_BIG, F32))
    m_scr[...] = jnp.max(m8, axis=0, keepdims=True)

    acc_scr[...] = jnp.zeros(acc_scr.shape, F32)

    def pv_step(r0, rows, c, carry):
        kvt = ckvt_ref[c, :, 0:rows]
        for hp in range(n_pair):
            cols = slice(hp * pair, (hp + 1) * pair)
            p = jnp.exp2(s_scr[pl.ds(r0, rows), cols] - m_scr[:, cols])
            acc_scr[:, cols] += _dot(kvt, p.astype(BF16))
        return carry

    over_keys(pv_step, 0)

    for hh in range(N_HEADS):
        cols = slice(hh * n_q, (hh + 1) * n_q)
        inv_l = 1.0 / acc_scr[KV_LORA:KV_LORA + 1, cols]
        olat = (acc_scr[0:KV_LORA, cols] * inv_l).astype(BF16)
        ot_scr[hh * V_DIM:(hh + 1) * V_DIM, :] = _dot(wuvt_ref[hh], olat)
    o_ref[...] = ot_scr[...].T.astype(BF16)


def _kernel_c(x_ref, o_ref, ma_ref, gb_ref, wao_ref, wo_ref, mng_ref, w1_ref, w2_ref, fng_ref,
              out_ref, *, ff_step):
    yb = _dot(o_ref[...], wao_ref[...])
    merged = ma_ref[...].astype(F32) + gb_ref[...].astype(F32) * yb
    x1 = x_ref[...] + _dot(merged.astype(BF16), wo_ref[...])
    hm = _rms(x1, mng_ref[...]).astype(BF16)
    acc = x1
    for j in range(D_FF // ff_step):
        h1 = jnp.maximum(_dot(hm, w1_ref[:, j * ff_step:(j + 1) * ff_step]), 0.0)
        acc = acc + _dot((h1 * h1).astype(BF16), w2_ref[j * ff_step:(j + 1) * ff_step, :])
    out_ref[...] = _rms(acc, fng_ref[...])


def _const_spec(shape):
    nd = len(shape)
    return pl.BlockSpec(shape, lambda *_: (0,) * nd, pipeline_mode=pl.Buffered(1))


def _params(semantics):
    return pltpu.CompilerParams(dimension_semantics=semantics,
                                vmem_limit_bytes=V7X_VMEM_LIMIT_BYTES)


def _layer(x, attn_norm_g, w_in, b_gate, dw_w, dw_b, conv_ln_g, conv_ln_b, w_conv_out, q_norm_g,
           kv_norm_g, w_uq, w_uk, w_uv, w_qi, kidx_ln_g, kidx_ln_b, w_attn_out, w_o, mlp_norm_g,
           w_ff1, w_ff2, out_norm_g):
    bsz, seq_len, _ = x.shape
    n_tok = bsz * seq_len
    top_k = min(IDX_TOPK_MAX, seq_len // 4)
    tm_a = min(256, seq_len)
    tm_c = min(512, seq_len)
    assert seq_len % Q_BLOCK == 0 and seq_len % tm_a == 0 and seq_len % tm_c == 0
    assert seq_len % KEY_STEP == 0

    row = lambda v: v.reshape(1, -1).astype(F32)
    x2 = x.reshape(n_tok, D_MODEL)

    o_cq = 2 * D_MODEL
    o_ckv = o_cq + Q_LORA
    o_kidx = o_ckv + KV_LORA
    o_widx = o_kidx + IDX_DIM
    o_gate = o_widx + IDX_HEADS
    wbig = jnp.concatenate([w_in[:, 0:2 * D_MODEL], w_in[:, o_gate:o_gate + 2 * D_MODEL]],
                           axis=1).astype(BF16)
    n_small = o_gate - o_cq
    wsm = jnp.pad(w_in[:, o_cq:o_gate], ((0, 0), (0, 4 * V7X_LANES - n_small))).astype(BF16)

    dww = jnp.broadcast_to(dw_w.astype(F32)[:, None, :], (CONV_KERNEL, V7X_SUBLANES, D_MODEL))
    klng = jnp.pad(kidx_ln_g, (0, V7X_LANES - IDX_DIM)).reshape(1, -1).astype(F32)
    klnb = jnp.pad(kidx_ln_b, (0, V7X_LANES - IDX_DIM)).reshape(1, -1).astype(F32)

    hpg = 4
    wuk_g = w_uk.reshape(N_HEADS // hpg, hpg, HEAD_DIM, KV_LORA)
    eye = jnp.eye(hpg, dtype=w_uk.dtype)
    wukbd = jnp.einsum("ghdc,hk->ghdkc", wuk_g, eye).reshape(
        N_HEADS // hpg, hpg * HEAD_DIM, hpg * KV_LORA).astype(BF16)

    tiles_per_seq = seq_len // tm_a
    tok_spec = lambda tm, n: pl.BlockSpec((tm, n), lambda t: (t, 0))
    a_in = [x2, row(attn_norm_g), wbig, wsm, row(b_gate), dww, row(dw_b), row(conv_ln_g),
            row(conv_ln_b), w_conv_out.astype(BF16), row(q_norm_g), row(kv_norm_g),
            w_uq.astype(BF16), wukbd, w_qi.astype(BF16), klng, klnb]
    a_specs = [tok_spec(tm_a, D_MODEL)] + [_const_spec(a.shape) for a in a_in[1:]]
    n_lat = N_HEADS * KV_LORA
    n_qidx = IDX_HEADS * IDX_DIM
    n_kstep = seq_len // KEY_STEP
    tiles_per_kstep = KEY_STEP // tm_a
    ma, gb, qlat, qidx, ckv, ckvt, kidx, widx_t = pl.pallas_call(
        functools.partial(_kernel_a, tm=tm_a, tiles_per_seq=tiles_per_seq),
        grid=(n_tok // tm_a,),
        in_specs=a_specs,
        out_specs=[tok_spec(tm_a, D_MODEL), tok_spec(tm_a, D_MODEL), tok_spec(tm_a, n_lat),
                   tok_spec(tm_a, n_qidx), tok_spec(tm_a, KV_LORA),
                   pl.BlockSpec((None, None, PV_ROWS, tm_a),
                                lambda t: (t // tiles_per_seq, (t % tiles_per_seq) // tiles_per_kstep, 0,
                                           t % tiles_per_kstep)),
                   tok_spec(tm_a, V7X_LANES),
                   pl.BlockSpec((None, IDX_HEADS, tm_a), lambda t: (t // tiles_per_seq, 0, t % tiles_per_seq))],
        out_shape=[jax.ShapeDtypeStruct((n_tok, D_MODEL), BF16),
                   jax.ShapeDtypeStruct((n_tok, D_MODEL), BF16),
                   jax.ShapeDtypeStruct((n_tok, n_lat), BF16),
                   jax.ShapeDtypeStruct((n_tok, n_qidx), BF16),
                   jax.ShapeDtypeStruct((n_tok, KV_LORA), BF16),
                   jax.ShapeDtypeStruct((bsz, n_kstep, PV_ROWS, KEY_STEP), BF16),
                   jax.ShapeDtypeStruct((n_tok, V7X_LANES), BF16),
                   jax.ShapeDtypeStruct((bsz, IDX_HEADS, seq_len), F32)],
        scratch_shapes=[pltpu.VMEM((V7X_SUBLANES, D_MODEL // V7X_LANES, CONV_HALO + tm_a, V7X_LANES), F32),
                        pltpu.VMEM((tm_a, D_MODEL), F32),
                        pltpu.VMEM((tm_a, D_MODEL), BF16),
                        pltpu.VMEM((tm_a, D_MODEL), F32)],
        compiler_params=_params(("arbitrary",)),
        name="block_in",
    )(*a_in)

    ckv = ckv.reshape(bsz, seq_len, KV_LORA)
    kidx = kidx.reshape(bsz, seq_len, V7X_LANES)
    wuvt = jnp.swapaxes(w_uv, 1, 2).astype(BF16)

    n_qb = seq_len // Q_BLOCK
    seq_spec = lambda r, c: pl.BlockSpec((None, r, c), lambda b, i: (b, 0, 0))
    o = pl.pallas_call(
        functools.partial(_kernel_b, seq_len=seq_len, top_k=top_k),
        grid=(bsz, n_qb),
        in_specs=[seq_spec(seq_len, V7X_LANES), seq_spec(seq_len, KV_LORA),
                  pl.BlockSpec((None, n_kstep, PV_ROWS, KEY_STEP), lambda b, i: (b, 0, 0, 0)),
                  pl.BlockSpec((None, IDX_HEADS, Q_BLOCK), lambda b, i: (b, 0, i)),
                  pl.BlockSpec((Q_BLOCK, n_qidx), lambda b, i: (b * n_qb + i, 0)),
                  pl.BlockSpec((Q_BLOCK, n_lat), lambda b, i: (b * n_qb + i, 0)),
                  pl.BlockSpec(wuvt.shape, lambda b, i: (0, 0, 0))],
        out_specs=pl.BlockSpec((Q_BLOCK, N_HEADS * V_DIM), lambda b, i: (b * n_qb + i, 0)),
        out_shape=jax.ShapeDtypeStruct((n_tok, N_HEADS * V_DIM), BF16),
        scratch_shapes=[pltpu.VMEM((IDX_HEADS * Q_BLOCK, V7X_LANES), BF16),
                        pltpu.VMEM((N_HEADS * Q_BLOCK, KV_LORA + Q_BLOCK), BF16),
                        pltpu.VMEM((seq_len, Q_BLOCK), I32),
                        pltpu.VMEM((seq_len, N_HEADS * Q_BLOCK), F32),
                        pltpu.VMEM((1, N_HEADS * Q_BLOCK), F32),
                        pltpu.VMEM((PV_ROWS, N_HEADS * Q_BLOCK), F32),
                        pltpu.VMEM((N_HEADS * V_DIM, Q_BLOCK), F32)],
        compiler_params=_params(("arbitrary", "arbitrary")),
        name="sparse_attn",
    )(kidx, ckv, ckvt, widx_t, qidx, qlat, wuvt)

    c_in = [x2, o, ma, gb, w_attn_out.astype(BF16), w_o.astype(BF16), row(mlp_norm_g),
            w_ff1.astype(BF16), w_ff2.astype(BF16), row(out_norm_g)]
    c_specs = [tok_spec(tm_c, D_MODEL)] * 4 + [_const_spec(a.shape) for a in c_in[4:]]
    out = pl.pallas_call(
        functools.partial(_kernel_c, ff_step=D_MODEL),
        grid=(n_tok // tm_c,),
        in_specs=c_specs,
        out_specs=tok_spec(tm_c, D_MODEL),
        out_shape=jax.ShapeDtypeStruct((n_tok, D_MODEL), F32),
        compiler_params=_params(("arbitrary",)),
        name="block_out",
    )(*c_in)
    return out.reshape(bsz, seq_len, D_MODEL)


def kernel(x, attn_norm_g, w_in, b_gate, dw_w, dw_b, conv_ln_g, conv_ln_b, w_conv_out, q_norm_g,
           kv_norm_g, w_uq, w_uk, w_uv, w_qi, kidx_ln_g, kidx_ln_b, w_attn_out, w_o, mlp_norm_g,
           w_ff1, w_ff2, final_norm_g):
    assert attn_norm_g.shape[0] == 1, "one layer: the final RMSNorm is fused into its output kernel"
    return _layer(x, attn_norm_g[0], w_in[0], b_gate[0], dw_w[0], dw_b[0], conv_ln_g[0],
                  conv_ln_b[0], w_conv_out[0], q_norm_g[0], kv_norm_g[0], w_uq[0], w_uk[0], w_uv[0],
                  w_qi[0], kidx_ln_g[0], kidx_ln_b[0], w_attn_out[0], w_o[0], mlp_norm_g[0],
                  w_ff1[0], w_ff2[0], final_norm_g)
```

```python
import functools

import jax
import jax.numpy as jnp
from jax import lax
from jax.experimental import pallas as pl
from jax.experimental.pallas import tpu as pltpu

D_MODEL = 1024
CHUNK = 64
Q_BLOCK = 128
EPS = 1e-6
CONV_KERNEL = 31
N_HEADS = 16
HEAD_DIM = 64
V_DIM = 64
Q_LORA = 256
KV_LORA = 128
IDX_HEADS = 8
IDX_DIM = 64
IDX_TOPK_MAX = 256
D_FF = 4 * D_MODEL

V7X_LANES = 128
V7X_SUBLANES = 8
V7X_VMEM_LIMIT_BYTES = 60000 * 1024
MXU_COLS = 256

CONV_HALO = 32
CONV_ROWS = 64
NORM_ROWS = 64
KEY_STEP = 4 * Q_BLOCK
HALF_STEP = KEY_STEP // 2
SCORE_ROWS = Q_BLOCK
NEG_BIG = -1e30
INT_MIN = -(2 ** 31)
LOG2_E = 1.4426950408889634
BISECT_GROUP = 4
BISECT_UNCHECKED = 20
V7X_BF16_ROWS = 16
PV_ROWS = KV_LORA + V7X_BF16_ROWS

F32 = jnp.float32
BF16 = jnp.bfloat16
I32 = jnp.int32


def _dot(a, b):
    return jnp.dot(a, b, preferred_element_type=F32)


def _dot_nt(a, b):
    return lax.dot_general(a, b, (((1,), (1,)), ((), ())), preferred_element_type=F32)


def _rms(x, g):
    ms = jnp.mean(x * x, axis=-1, keepdims=True)
    return x * lax.rsqrt(ms + EPS) * g


def _sigmoid(x):
    return 1.0 / (1.0 + jnp.exp(-x))


def _kernel_a(x_ref, ng_ref, wbig_ref, wsm_ref, bgate_ref, dww_ref, dwb_ref, lng_ref, lnb_ref,
              wpw_ref, qng_ref, kvng_ref, wuq_ref, wukbd_ref, wqi_ref, klng_ref, klnb_ref,
              ma_ref, gb_ref, qlat_ref, qidx_ref, ckv_ref, ckvt_ref, kidx_ref, widx_ref,
              vbuf, ybuf, zbuf, gabuf, *, tm, tiles_per_seq):
    t = pl.program_id(0)
    first = (t % tiles_per_seq) == 0

    x = x_ref[...]
    h = _rms(x, ng_ref[...]).astype(BF16)

    n_lt = D_MODEL // V7X_LANES

    @pl.when(first)
    def _():
        vbuf[:, :, 0:CONV_HALO, :] = jnp.zeros((V7X_SUBLANES, n_lt, CONV_HALO, V7X_LANES), F32)

    @pl.when(jnp.logical_not(first))
    def _():
        vbuf[:, :, 0:CONV_HALO, :] = vbuf[:, :, tm:tm + CONV_HALO, :]

    for cb in range(D_MODEL // MXU_COLS):
        c0 = cb * MXU_COLS
        ua = _dot(h, wbig_ref[:, c0:c0 + MXU_COLS])
        ug = _dot(h, wbig_ref[:, D_MODEL + c0:D_MODEL + c0 + MXU_COLS])
        v = ua * _sigmoid(ug)
        for s in range(V7X_SUBLANES):
            for l2 in range(MXU_COLS // V7X_LANES):
                lt = c0 // V7X_LANES + l2
                vbuf[s, lt, CONV_HALO - s:CONV_HALO - s + tm, :] = v[:, l2 * V7X_LANES:(l2 + 1) * V7X_LANES]

    tap0 = CONV_HALO - (CONV_KERNEL - 1)

    row_groups = CONV_ROWS // V7X_SUBLANES

    def conv_chunk(r, carry):
        base = r * CONV_ROWS
        for lt in range(n_lt):
            lanes = slice(lt * V7X_LANES, (lt + 1) * V7X_LANES)
            acc = jnp.zeros((row_groups, V7X_SUBLANES, V7X_LANES), F32)
            for k in range(CONV_KERNEL):
                j, s = divmod(tap0 + k, V7X_SUBLANES)
                row0 = base + V7X_SUBLANES * j
                win = vbuf[s, lt, pl.ds(row0, CONV_ROWS), :]
                acc = acc + dww_ref[k, :, lanes][None] * win.reshape(row_groups, V7X_SUBLANES, V7X_LANES)
            ybuf[pl.ds(base, CONV_ROWS), lanes] = acc.reshape(CONV_ROWS, V7X_LANES)
        return carry

    assert tm // CONV_ROWS == 4 and NORM_ROWS == CONV_ROWS
    def norm_chunk(r, carry):
        base = r * NORM_ROWS
        y = ybuf[pl.ds(base, NORM_ROWS), :] + dwb_ref[...]
        mu = jnp.mean(y, axis=-1, keepdims=True)
        yc = y - mu
        var = jnp.mean(yc * yc, axis=-1, keepdims=True)
        z = yc * lax.rsqrt(var + EPS) * lng_ref[...] + lnb_ref[...]
        z = z * _sigmoid(z)
        zbuf[pl.ds(base, NORM_ROWS), :] = z.astype(BF16)
        return carry

    conv_chunk(0, 0)
    for cb in range(D_MODEL // MXU_COLS):
        c0 = cb * MXU_COLS
        gabuf[:, c0:c0 + MXU_COLS] = _sigmoid(
            _dot(h, wbig_ref[:, 2 * D_MODEL + c0:2 * D_MODEL + c0 + MXU_COLS]) + bgate_ref[:, c0:c0 + MXU_COLS])
    conv_chunk(1, 0)
    norm_chunk(0, 0)
    for cb in range(D_MODEL // MXU_COLS):
        c0 = cb * MXU_COLS
        gb = _sigmoid(_dot(h, wbig_ref[:, 3 * D_MODEL + c0:3 * D_MODEL + c0 + MXU_COLS])
                      + bgate_ref[:, D_MODEL + c0:D_MODEL + c0 + MXU_COLS])
        gb_ref[:, c0:c0 + MXU_COLS] = gb.astype(BF16)

    conv_chunk(2, 0)
    norm_chunk(1, 0)
    us = _dot(h, wsm_ref[...])
    cq = _rms(us[:, 0:Q_LORA], qng_ref[...]).astype(BF16)
    ckv = _rms(us[:, Q_LORA:Q_LORA + KV_LORA], kvng_ref[...])
    ckv_ref[...] = ckv.astype(BF16)
    ckvt_ref[0:KV_LORA, :] = ckv.T.astype(BF16)
    ckvt_ref[KV_LORA:PV_ROWS, :] = jnp.where(
        lax.broadcasted_iota(I32, (PV_ROWS - KV_LORA, tm), 0) == 0, 1.0, 0.0).astype(BF16)

    t3 = us[:, Q_LORA + KV_LORA:Q_LORA + KV_LORA + V7X_LANES]
    lane = lax.broadcasted_iota(I32, t3.shape, 1)
    is_k = lane < IDX_DIM
    mu = jnp.sum(jnp.where(is_k, t3, 0.0), axis=-1, keepdims=True) * (1.0 / IDX_DIM)
    kc = jnp.where(is_k, t3 - mu, 0.0)
    var = jnp.sum(kc * kc, axis=-1, keepdims=True) * (1.0 / IDX_DIM)
    kn = kc * lax.rsqrt(var + EPS) * klng_ref[...] + klnb_ref[...]
    kidx_ref[...] = (kn + pltpu.roll(kn, IDX_DIM, 1)).astype(BF16)
    wi = t3 * (IDX_HEADS ** -0.5 * IDX_DIM ** -0.5)
    widx_ref[...] = wi.T[IDX_DIM:IDX_DIM + IDX_HEADS, :]

    conv_chunk(3, 0)
    norm_chunk(2, 0)
    q = _dot(cq, wuq_ref[...]).astype(BF16)
    heads_per_group = wukbd_ref.shape[1] // HEAD_DIM
    lat_per_group = heads_per_group * KV_LORA
    for g in range(N_HEADS // heads_per_group):
        ql = _dot(q[:, g * heads_per_group * HEAD_DIM:(g + 1) * heads_per_group * HEAD_DIM],
                  wukbd_ref[g])
        qlat_ref[:, g * lat_per_group:(g + 1) * lat_per_group] = (ql * (HEAD_DIM ** -0.5 * LOG2_E)).astype(BF16)
    qidx_ref[...] = _dot(cq, wqi_ref[...]).astype(BF16)

    norm_chunk(3, 0)
    ya = _dot(zbuf[...], wpw_ref[...])
    ma_ref[...] = (gabuf[...] * ya).astype(BF16)


def _kernel_b(kidx_ref, ckv_ref, ckvt_ref, widx_ref, qidx_ref, qlat_ref, wuvt_ref, o_ref,
              qall_scr, qlat_scr, key_scr, s_scr, m_scr, acc_scr, ot_scr,
              *, seq_len, top_k):
    i = pl.program_id(1)
    n_q = Q_BLOCK
    blk_per_step = KEY_STEP // Q_BLOCK
    rem = (i + 1) % blk_per_step
    has_half = jnp.logical_and(rem > 0, rem <= HALF_STEP // Q_BLOCK)
    n_full = (i + 1) // blk_per_step + jnp.where(rem > HALF_STEP // Q_BLOCK, 1, 0)
    tail0 = pl.multiple_of(n_full * KEY_STEP, KEY_STEP)

    def over_keys(step_fn, init):
        def full(c, carry):
            return step_fn(pl.multiple_of(c * KEY_STEP, KEY_STEP), KEY_STEP, c, carry)
        carry = lax.fori_loop(0, n_full, full, init)
        return lax.cond(has_half, lambda cr: step_fn(tail0, HALF_STEP, n_full, cr), lambda cr: cr, carry)

    lane = lax.broadcasted_iota(I32, (n_q, V7X_LANES), 1)
    for hh in range(IDX_HEADS):
        tile = qidx_ref[:, (hh // 2) * V7X_LANES:(hh // 2 + 1) * V7X_LANES].astype(F32)
        keep = (lane < IDX_DIM) if hh % 2 == 0 else (lane >= IDX_DIM)
        qall_scr[hh * n_q:(hh + 1) * n_q, :] = jnp.where(keep, tile, 0.0).astype(BF16)
    for hh in range(N_HEADS):
        qlat_scr[hh * n_q:(hh + 1) * n_q, 0:KV_LORA] = qlat_ref[:, hh * KV_LORA:(hh + 1) * KV_LORA]

    @pl.when(i == 0)
    def _():
        eye = (lax.broadcasted_iota(I32, (n_q, n_q), 0) == lax.broadcasted_iota(I32, (n_q, n_q), 1))
        eye = jnp.where(eye, 1.0, 0.0).astype(BF16)
        for hh in range(N_HEADS):
            qlat_scr[hh * n_q:(hh + 1) * n_q, KV_LORA:KV_LORA + n_q] = eye

    qabs_chunk = (i * n_q + lax.broadcasted_iota(I32, (SCORE_ROWS, n_q), 1)) // CHUNK

    def score_step(first, rows, c, carry):
        for sub in range(rows // SCORE_ROWS):
            r0 = pl.multiple_of(first + sub * SCORE_ROWS, SCORE_ROWS)
            kx = kidx_ref[pl.ds(r0, SCORE_ROWS), :]
            sc = jnp.zeros((SCORE_ROWS, n_q), F32)
            for hp in range(IDX_HEADS // 2):
                lg = _dot_nt(kx, qall_scr[hp * MXU_COLS:(hp + 1) * MXU_COLS, :])
                for h2 in range(2):
                    hh = 2 * hp + h2
                    sc = sc + widx_ref[hh:hh + 1, :] * jnp.maximum(lg[:, h2 * n_q:(h2 + 1) * n_q], 0.0)
            bits = pltpu.bitcast(sc, I32)
            key = bits ^ ((bits >> 31) & 0x7FFFFFFF)
            key = jnp.where(bits == INT_MIN, 0, key)
            kabs_chunk = (r0 + lax.broadcasted_iota(I32, (SCORE_ROWS, n_q), 0)) // CHUNK
            key_scr[pl.ds(r0, SCORE_ROWS), :] = jnp.where(kabs_chunk <= qabs_chunk, key, INT_MIN)
        return carry

    over_keys(score_step, 0)

    def count_rows(pred):
        def body(r0, rows, c, cnt):
            hit = pred(key_scr[pl.ds(r0, rows), :], r0)
            ones = jnp.where(hit, 1, 0).astype(I32)
            return cnt + jnp.sum(ones.reshape(rows // V7X_SUBLANES, V7X_SUBLANES, n_q), axis=0)
        cnt8 = over_keys(body, jnp.zeros((V7X_SUBLANES, n_q), I32))
        return jnp.sum(cnt8, axis=0, keepdims=True)

    def bisect_bits(it, cur, settled, n_bits):
        for j in range(n_bits):
            cand = cur ^ jnp.left_shift(jnp.int32(1), 31 - (it + j))
            cnt = count_rows(lambda k, r0: k >= cand)
            step = jnp.where(cnt >= top_k, cand, cur)
            cur = jnp.where(settled > 0, cur, step)
            settled = jnp.maximum(settled, jnp.where(cnt == top_k, 1, 0))
        return cur, settled

    def open_count(settled):
        return n_q - jnp.sum(settled.astype(F32))

    def unchecked_group(g, state):
        return bisect_bits(g * BISECT_GROUP, *state, BISECT_GROUP)

    def bisect_group(state):
        it, cur, settled, _ = state
        cur, settled = bisect_bits(it, cur, settled, BISECT_GROUP)
        return it + BISECT_GROUP, cur, settled, open_count(settled)

    def bisect_more(state):
        it, _, _, open_lanes = state
        return jnp.logical_and(it < 32, open_lanes > 0.0)

    def search_threshold():
        cur0, settled0 = lax.fori_loop(
            0, BISECT_UNCHECKED // BISECT_GROUP, unchecked_group,
            (jnp.full((1, n_q), INT_MIN, I32), jnp.zeros((1, n_q), I32)))
        _, cur, _, still_open = lax.while_loop(
            bisect_more, bisect_group, (jnp.int32(BISECT_UNCHECKED), cur0, settled0, open_count(settled0)))
        return cur, still_open

    thr, open_lanes = lax.cond((i + 1) * Q_BLOCK > top_k, search_threshold,
                               lambda: (jnp.full((1, n_q), INT_MIN, I32), jnp.float32(0.0)))
    thr = jnp.maximum(thr, INT_MIN + 1)
    all_rows = lambda: jnp.full((1, n_q), seq_len, I32)

    def tie_rows():
        n_gt = count_rows(lambda k, r0: k > thr)
        n_ge = count_rows(lambda k, r0: k >= thr)
        need = top_k - n_gt

        def tie_limit():
            n_bits = max(1, (seq_len - 1).bit_length())

            def bisect_row(it, cur):
                cand = cur + jnp.left_shift(jnp.int32(1), n_bits - 1 - it)

                def pred(k, r0):
                    row = r0 + lax.broadcasted_iota(I32, k.shape, 0)
                    return jnp.logical_and(k == thr, row < cand)
                cnt = count_rows(pred)
                return jnp.where(cnt < need, cand, cur)

            return lax.fori_loop(0, n_bits, bisect_row, jnp.zeros((1, n_q), I32))

        return lax.cond(jnp.max(n_ge.astype(F32)) > top_k, tie_limit, all_rows)

    row_lim = lax.cond(open_lanes > 0.0, tie_rows, all_rows)

    pair = 2 * n_q
    n_pair = N_HEADS // 2
    n_col = N_HEADS * n_q

    def qk_step(r0, rows, c, m8):
        k = key_scr[pl.ds(r0, rows), :]
        row = r0 + lax.broadcasted_iota(I32, (rows, n_q), 0)
        take = jnp.logical_or(k > thr, jnp.logical_and(k == thr, row <= row_lim))
        mask = jnp.where(take, 0.0, NEG_BIG).astype(BF16)
        kvb = jnp.concatenate([ckv_ref[pl.ds(r0, rows), :], mask], axis=1)
        parts = []
        for hp in range(n_pair):
            s = _dot_nt(kvb, qlat_scr[hp * pair:(hp + 1) * pair, :])
            s_scr[hp, pl.ds(r0, rows), :] = s
            parts.append(jnp.max(s.reshape(rows // V7X_SUBLANES, V7X_SUBLANES, pair), axis=0))
        return jnp.maximum(m8, jnp.concatenate(parts, axis=1))

    m8 = over_keys(qk_step, jnp.full((V7X_SUBLANES, n_col), NEG_BIG, F32))
    m_scr[...] = jnp.max(m8, axis=0, keepdims=True)

    acc_scr[...] = jnp.zeros(acc_scr.shape, F32)

    def pv_step(r0, rows, c, carry):
        kvt = ckvt_ref[c, :, 0:rows]
        for hp in range(n_pair):
            cols = slice(hp * pair, (hp + 1) * pair)
            p = jnp.exp2(s_scr[hp, pl.ds(r0, rows), :] - m_scr[:, cols])
            acc_scr[:, cols] += _dot(kvt, p.astype(BF16))
        return carry

    over_keys(pv_step, 0)

    for hh in range(N_HEADS):
        cols = slice(hh * n_q, (hh + 1) * n_q)
        inv_l = 1.0 / acc_scr[KV_LORA:KV_LORA + 1, cols]
        olat = (acc_scr[0:KV_LORA, cols] * inv_l).astype(BF16)
        ot_scr[hh * V_DIM:(hh + 1) * V_DIM, :] = _dot(wuvt_ref[hh], olat)
    o_ref[...] = ot_scr[...].T.astype(BF16)


def _kernel_c(x_ref, o_ref, ma_ref, gb_ref, wao_ref, wo_ref, mng_ref, w1_ref, w2_ref, fng_ref,
              out_ref, *, ff_step):
    yb = _dot(o_ref[...], wao_ref[...])
    merged = ma_ref[...].astype(F32) + gb_ref[...].astype(F32) * yb
    x1 = x_ref[...] + _dot(merged.astype(BF16), wo_ref[...])
    hm = _rms(x1, mng_ref[...]).astype(BF16)
    acc = x1
    for j in range(D_FF // ff_step):
        h1 = jnp.maximum(_dot(hm, w1_ref[:, j * ff_step:(j + 1) * ff_step]), 0.0)
        acc = acc + _dot((h1 * h1).astype(BF16), w2_ref[j * ff_step:(j + 1) * ff_step, :])
    out_ref[...] = _rms(acc, fng_ref[...])


def _const_spec(shape):
    nd = len(shape)
    return pl.BlockSpec(shape, lambda *_: (0,) * nd, pipeline_mode=pl.Buffered(1))


def _params(semantics):
    return pltpu.CompilerParams(dimension_semantics=semantics,
                                vmem_limit_bytes=V7X_VMEM_LIMIT_BYTES)


def _layer(x, attn_norm_g, w_in, b_gate, dw_w, dw_b, conv_ln_g, conv_ln_b, w_conv_out, q_norm_g,
           kv_norm_g, w_uq, w_uk, w_uv, w_qi, kidx_ln_g, kidx_ln_b, w_attn_out, w_o, mlp_norm_g,
           w_ff1, w_ff2, out_norm_g):
    bsz, seq_len, _ = x.shape
    n_tok = bsz * seq_len
    top_k = min(IDX_TOPK_MAX, seq_len // 4)
    tm_a = min(256, seq_len)
    tm_c = min(512, seq_len)
    assert seq_len % Q_BLOCK == 0 and seq_len % tm_a == 0 and seq_len % tm_c == 0
    assert seq_len % KEY_STEP == 0

    row = lambda v: v.reshape(1, -1).astype(F32)
    x2 = x.reshape(n_tok, D_MODEL)

    o_cq = 2 * D_MODEL
    o_ckv = o_cq + Q_LORA
    o_kidx = o_ckv + KV_LORA
    o_widx = o_kidx + IDX_DIM
    o_gate = o_widx + IDX_HEADS
    wbig = jnp.concatenate([w_in[:, 0:2 * D_MODEL], w_in[:, o_gate:o_gate + 2 * D_MODEL]],
                           axis=1).astype(BF16)
    n_small = o_gate - o_cq
    wsm = jnp.pad(w_in[:, o_cq:o_gate], ((0, 0), (0, 4 * V7X_LANES - n_small))).astype(BF16)

    dww = jnp.broadcast_to(dw_w.astype(F32)[:, None, :], (CONV_KERNEL, V7X_SUBLANES, D_MODEL))
    klng = jnp.pad(kidx_ln_g, (0, V7X_LANES - IDX_DIM)).reshape(1, -1).astype(F32)
    klnb = jnp.pad(kidx_ln_b, (0, V7X_LANES - IDX_DIM)).reshape(1, -1).astype(F32)

    hpg = 4
    wuk_g = w_uk.reshape(N_HEADS // hpg, hpg, HEAD_DIM, KV_LORA)
    eye = jnp.eye(hpg, dtype=w_uk.dtype)
    wukbd = jnp.einsum("ghdc,hk->ghdkc", wuk_g, eye).reshape(
        N_HEADS // hpg, hpg * HEAD_DIM, hpg * KV_LORA).astype(BF16)

    tiles_per_seq = seq_len // tm_a
    tok_spec = lambda tm, n: pl.BlockSpec((tm, n), lambda t: (t, 0))
    a_in = [x2, row(attn_norm_g), wbig, wsm, row(b_gate), dww, row(dw_b), row(conv_ln_g),
            row(conv_ln_b), w_conv_out.astype(BF16), row(q_norm_g), row(kv_norm_g),
            w_uq.astype(BF16), wukbd, w_qi.astype(BF16), klng, klnb]
    a_specs = [tok_spec(tm_a, D_MODEL)] + [_const_spec(a.shape) for a in a_in[1:]]
    n_lat = N_HEADS * KV_LORA
    n_qidx = IDX_HEADS * IDX_DIM
    n_kstep = seq_len // KEY_STEP
    tiles_per_kstep = KEY_STEP // tm_a
    ma, gb, qlat, qidx, ckv, ckvt, kidx, widx_t = pl.pallas_call(
        functools.partial(_kernel_a, tm=tm_a, tiles_per_seq=tiles_per_seq),
        grid=(n_tok // tm_a,),
        in_specs=a_specs,
        out_specs=[tok_spec(tm_a, D_MODEL), tok_spec(tm_a, D_MODEL), tok_spec(tm_a, n_lat),
                   tok_spec(tm_a, n_qidx), tok_spec(tm_a, KV_LORA),
                   pl.BlockSpec((None, None, PV_ROWS, tm_a),
                                lambda t: (t // tiles_per_seq, (t % tiles_per_seq) // tiles_per_kstep, 0,
                                           t % tiles_per_kstep)),
                   tok_spec(tm_a, V7X_LANES),
                   pl.BlockSpec((None, IDX_HEADS, tm_a), lambda t: (t // tiles_per_seq, 0, t % tiles_per_seq))],
        out_shape=[jax.ShapeDtypeStruct((n_tok, D_MODEL), BF16),
                   jax.ShapeDtypeStruct((n_tok, D_MODEL), BF16),
                   jax.ShapeDtypeStruct((n_tok, n_lat), BF16),
                   jax.ShapeDtypeStruct((n_tok, n_qidx), BF16),
                   jax.ShapeDtypeStruct((n_tok, KV_LORA), BF16),
                   jax.ShapeDtypeStruct((bsz, n_kstep, PV_ROWS, KEY_STEP), BF16),
                   jax.ShapeDtypeStruct((n_tok, V7X_LANES), BF16),
                   jax.ShapeDtypeStruct((bsz, IDX_HEADS, seq_len), F32)],
        scratch_shapes=[pltpu.VMEM((V7X_SUBLANES, D_MODEL // V7X_LANES, CONV_HALO + tm_a, V7X_LANES), F32),
                        pltpu.VMEM((tm_a, D_MODEL), F32),
                        pltpu.VMEM((tm_a, D_MODEL), BF16),
                        pltpu.VMEM((tm_a, D_MODEL), F32)],
        compiler_params=_params(("arbitrary",)),
        name="block_in",
    )(*a_in)

    ckv = ckv.reshape(bsz, seq_len, KV_LORA)
    kidx = kidx.reshape(bsz, seq_len, V7X_LANES)
    wuvt = jnp.swapaxes(w_uv, 1, 2).astype(BF16)

    n_qb = seq_len // Q_BLOCK
    seq_spec = lambda r, c: pl.BlockSpec((None, r, c), lambda b, i: (b, 0, 0))
    o = pl.pallas_call(
        functools.partial(_kernel_b, seq_len=seq_len, top_k=top_k),
        grid=(bsz, n_qb),
        in_specs=[seq_spec(seq_len, V7X_LANES), seq_spec(seq_len, KV_LORA),
                  pl.BlockSpec((None, n_kstep, PV_ROWS, KEY_STEP), lambda b, i: (b, 0, 0, 0)),
                  pl.BlockSpec((None, IDX_HEADS, Q_BLOCK), lambda b, i: (b, 0, i)),
                  pl.BlockSpec((Q_BLOCK, n_qidx), lambda b, i: (b * n_qb + i, 0)),
                  pl.BlockSpec((Q_BLOCK, n_lat), lambda b, i: (b * n_qb + i, 0)),
                  pl.BlockSpec(wuvt.shape, lambda b, i: (0, 0, 0))],
        out_specs=pl.BlockSpec((Q_BLOCK, N_HEADS * V_DIM), lambda b, i: (b * n_qb + i, 0)),
        out_shape=jax.ShapeDtypeStruct((n_tok, N_HEADS * V_DIM), BF16),
        scratch_shapes=[pltpu.VMEM((IDX_HEADS * Q_BLOCK, V7X_LANES), BF16),
                        pltpu.VMEM((N_HEADS * Q_BLOCK, KV_LORA + Q_BLOCK), BF16),
                        pltpu.VMEM((seq_len, Q_BLOCK), I32),
                        pltpu.VMEM((N_HEADS // 2, seq_len, 2 * Q_BLOCK), F32),
                        pltpu.VMEM((1, N_HEADS * Q_BLOCK), F32),
                        pltpu.VMEM((PV_ROWS, N_HEADS * Q_BLOCK), F32),
                        pltpu.VMEM((N_HEADS * V_DIM, Q_BLOCK), F32)],
        compiler_params=_params(("arbitrary", "arbitrary")),
        name="sparse_attn",
    )(kidx, ckv, ckvt, widx_t, qidx, qlat, wuvt)

    c_in = [x2, o, ma, gb, w_attn_out.astype(BF16), w_o.astype(BF16), row(mlp_norm_g),
            w_ff1.astype(BF16), w_ff2.astype(BF16), row(out_norm_g)]
    c_specs = [tok_spec(tm_c, D_MODEL)] * 4 + [_const_spec(a.shape) for a in c_in[4:]]
    out = pl.pallas_call(
        functools.partial(_kernel_c, ff_step=D_MODEL),
        grid=(n_tok // tm_c,),
        in_specs=c_specs,
        out_specs=tok_spec(tm_c, D_MODEL),
        out_shape=jax.ShapeDtypeStruct((n_tok, D_MODEL), F32),
        compiler_params=_params(("arbitrary",)),
        name="block_out",
    )(*c_in)
    return out.reshape(bsz, seq_len, D_MODEL)


def kernel(x, attn_norm_g, w_in, b_gate, dw_w, dw_b, conv_ln_g, conv_ln_b, w_conv_out, q_norm_g,
           kv_norm_g, w_uq, w_uk, w_uv, w_qi, kidx_ln_g, kidx_ln_b, w_attn_out, w_o, mlp_norm_g,
           w_ff1, w_ff2, final_norm_g):
    assert attn_norm_g.shape[0] == 1, "one layer: the final RMSNorm is fused into its output kernel"
    return _layer(x, attn_norm_g[0], w_in[0], b_gate[0], dw_w[0], dw_b[0], conv_ln_g[0],
                  conv_ln_b[0], w_conv_out[0], q_norm_g[0], kv_norm_g[0], w_uq[0], w_uk[0], w_uv[0],
                  w_qi[0], kidx_ln_g[0], kidx_ln_b[0], w_attn_out[0], w_o[0], mlp_norm_g[0],
                  w_ff1[0], w_ff2[0], final_norm_g)
```

```python
import functools

import jax
import jax.numpy as jnp
from jax import lax
from jax.experimental import pallas as pl
from jax.experimental.pallas import tpu as pltpu

D_MODEL = 1024
CHUNK = 64
Q_BLOCK = 128
EPS = 1e-6
CONV_KERNEL = 31
N_HEADS = 16
HEAD_DIM = 64
V_DIM = 64
Q_LORA = 256
KV_LORA = 128
IDX_HEADS = 8
IDX_DIM = 64
IDX_TOPK_MAX = 256
D_FF = 4 * D_MODEL

V7X_LANES = 128
V7X_SUBLANES = 8
V7X_VMEM_LIMIT_BYTES = 60000 * 1024
MXU_COLS = 256

CONV_HALO = 32
CONV_ROWS = 64
NORM_ROWS = 64
KEY_STEP = 4 * Q_BLOCK
HALF_STEP = KEY_STEP // 2
SCORE_ROWS = Q_BLOCK
NEG_BIG = -1e30
INT_MIN = -(2 ** 31)
LOG2_E = 1.4426950408889634
BISECT_GROUP = 4
BISECT_UNCHECKED = 20
V7X_BF16_ROWS = 16
PV_ROWS = KV_LORA + V7X_BF16_ROWS

F32 = jnp.float32
BF16 = jnp.bfloat16
I32 = jnp.int32


def _dot(a, b):
    return jnp.dot(a, b, preferred_element_type=F32)


def _dot_nt(a, b):
    return lax.dot_general(a, b, (((1,), (1,)), ((), ())), preferred_element_type=F32)


def _rms(x, g):
    ms = jnp.mean(x * x, axis=-1, keepdims=True)
    return x * lax.rsqrt(ms + EPS) * g


def _sigmoid(x):
    return 1.0 / (1.0 + jnp.exp(-x))


def _kernel_a(x_ref, ng_ref, wbig_ref, wsm_ref, bgate_ref, dww_ref, dwb_ref, lng_ref, lnb_ref,
              wpw_ref, qng_ref, kvng_ref, wuq_ref, wukbd_ref, wqi_ref, klng_ref, klnb_ref,
              ma_ref, gb_ref, qlat_ref, qidx_ref, ckv_ref, ckvt_ref, kidx_ref, widx_ref,
              vbuf, ybuf, zbuf, gabuf, *, tm, tiles_per_seq):
    t = pl.program_id(0)
    first = (t % tiles_per_seq) == 0

    x = x_ref[...]
    h = _rms(x, ng_ref[...]).astype(BF16)

    n_lt = D_MODEL // V7X_LANES

    @pl.when(first)
    def _():
        vbuf[:, :, 0:CONV_HALO, :] = jnp.zeros((V7X_SUBLANES, n_lt, CONV_HALO, V7X_LANES), F32)

    @pl.when(jnp.logical_not(first))
    def _():
        vbuf[:, :, 0:CONV_HALO, :] = vbuf[:, :, tm:tm + CONV_HALO, :]

    for cb in range(D_MODEL // MXU_COLS):
        c0 = cb * MXU_COLS
        ua = _dot(h, wbig_ref[:, c0:c0 + MXU_COLS])
        ug = _dot(h, wbig_ref[:, D_MODEL + c0:D_MODEL + c0 + MXU_COLS])
        v = ua * _sigmoid(ug)
        for s in range(V7X_SUBLANES):
            for l2 in range(MXU_COLS // V7X_LANES):
                lt = c0 // V7X_LANES + l2
                vbuf[s, lt, CONV_HALO - s:CONV_HALO - s + tm, :] = v[:, l2 * V7X_LANES:(l2 + 1) * V7X_LANES]

    tap0 = CONV_HALO - (CONV_KERNEL - 1)

    row_groups = CONV_ROWS // V7X_SUBLANES

    def conv_chunk(r, lane_tiles):
        base = r * CONV_ROWS
        for lt in lane_tiles:
            lanes = slice(lt * V7X_LANES, (lt + 1) * V7X_LANES)
            acc = jnp.zeros((row_groups, V7X_SUBLANES, V7X_LANES), F32)
            for k in range(CONV_KERNEL):
                j, s = divmod(tap0 + k, V7X_SUBLANES)
                row0 = base + V7X_SUBLANES * j
                win = vbuf[s, lt, pl.ds(row0, CONV_ROWS), :]
                acc = acc + dww_ref[k, :, lanes][None] * win.reshape(row_groups, V7X_SUBLANES, V7X_LANES)
            ybuf[pl.ds(base, CONV_ROWS), lanes] = acc.reshape(CONV_ROWS, V7X_LANES)

    assert tm // CONV_ROWS == 4 and NORM_ROWS == CONV_ROWS
    def norm_chunk(r, carry):
        base = r * NORM_ROWS
        y = ybuf[pl.ds(base, NORM_ROWS), :] + dwb_ref[...]
        mu = jnp.mean(y, axis=-1, keepdims=True)
        yc = y - mu
        var = jnp.mean(yc * yc, axis=-1, keepdims=True)
        z = yc * lax.rsqrt(var + EPS) * lng_ref[...] + lnb_ref[...]
        z = z * _sigmoid(z)
        zbuf[pl.ds(base, NORM_ROWS), :] = z.astype(BF16)
        return carry

    def gate_a(cb):
        c0 = cb * MXU_COLS
        gabuf[:, c0:c0 + MXU_COLS] = _sigmoid(
            _dot(h, wbig_ref[:, 2 * D_MODEL + c0:2 * D_MODEL + c0 + MXU_COLS]) + bgate_ref[:, c0:c0 + MXU_COLS])

    def gate_b(cb):
        c0 = cb * MXU_COLS
        gb = _sigmoid(_dot(h, wbig_ref[:, 3 * D_MODEL + c0:3 * D_MODEL + c0 + MXU_COLS])
                      + bgate_ref[:, D_MODEL + c0:D_MODEL + c0 + MXU_COLS])
        gb_ref[:, c0:c0 + MXU_COLS] = gb.astype(BF16)

    lo_tiles, hi_tiles = range(0, n_lt // 2), range(n_lt // 2, n_lt)
    conv_chunk(0, lo_tiles)
    gate_a(0)
    conv_chunk(0, hi_tiles)
    gate_a(1)
    conv_chunk(1, lo_tiles)
    gate_a(2)
    conv_chunk(1, hi_tiles)
    gate_a(3)
    norm_chunk(0, 0)
    conv_chunk(2, lo_tiles)
    gate_b(0)
    conv_chunk(2, hi_tiles)
    gate_b(1)
    norm_chunk(1, 0)

    us = _dot(h, wsm_ref[...])
    cq = _rms(us[:, 0:Q_LORA], qng_ref[...]).astype(BF16)
    ckv = _rms(us[:, Q_LORA:Q_LORA + KV_LORA], kvng_ref[...])
    ckv_ref[...] = ckv.astype(BF16)
    ckvt_ref[0:KV_LORA, :] = ckv.T.astype(BF16)
    ckvt_ref[KV_LORA:PV_ROWS, :] = jnp.where(
        lax.broadcasted_iota(I32, (PV_ROWS - KV_LORA, tm), 0) == 0, 1.0, 0.0).astype(BF16)

    t3 = us[:, Q_LORA + KV_LORA:Q_LORA + KV_LORA + V7X_LANES]
    lane = lax.broadcasted_iota(I32, t3.shape, 1)
    is_k = lane < IDX_DIM
    mu = jnp.sum(jnp.where(is_k, t3, 0.0), axis=-1, keepdims=True) * (1.0 / IDX_DIM)
    kc = jnp.where(is_k, t3 - mu, 0.0)
    var = jnp.sum(kc * kc, axis=-1, keepdims=True) * (1.0 / IDX_DIM)
    kn = kc * lax.rsqrt(var + EPS) * klng_ref[...] + klnb_ref[...]
    kidx_ref[...] = (kn + pltpu.roll(kn, IDX_DIM, 1)).astype(BF16)
    wi = t3 * (IDX_HEADS ** -0.5 * IDX_DIM ** -0.5)
    widx_ref[...] = wi.T[IDX_DIM:IDX_DIM + IDX_HEADS, :]

    conv_chunk(3, lo_tiles)
    gate_b(2)
    conv_chunk(3, hi_tiles)
    gate_b(3)
    norm_chunk(2, 0)
    q = _dot(cq, wuq_ref[...]).astype(BF16)
    heads_per_group = wukbd_ref.shape[1] // HEAD_DIM
    lat_per_group = heads_per_group * KV_LORA
    for g in range(N_HEADS // heads_per_group):
        ql = _dot(q[:, g * heads_per_group * HEAD_DIM:(g + 1) * heads_per_group * HEAD_DIM],
                  wukbd_ref[g])
        qlat_ref[:, g * lat_per_group:(g + 1) * lat_per_group] = (ql * (HEAD_DIM ** -0.5 * LOG2_E)).astype(BF16)
    qidx_ref[...] = _dot(cq, wqi_ref[...]).astype(BF16)

    norm_chunk(3, 0)
    ya = _dot(zbuf[...], wpw_ref[...])
    ma_ref[...] = (gabuf[...] * ya).astype(BF16)


def _kernel_b(kidx_ref, ckv_ref, ckvt_ref, widx_ref, qidx_ref, qlat_ref, wuvt_ref, o_ref,
              qall_scr, qlat_scr, key_scr, s_scr, m_scr, acc_scr, ot_scr,
              *, seq_len, top_k):
    i = pl.program_id(1)
    n_q = Q_BLOCK
    blk_per_step = KEY_STEP // Q_BLOCK
    rem = (i + 1) % blk_per_step
    has_half = jnp.logical_and(rem > 0, rem <= HALF_STEP // Q_BLOCK)
    n_full = (i + 1) // blk_per_step + jnp.where(rem > HALF_STEP // Q_BLOCK, 1, 0)
    tail0 = pl.multiple_of(n_full * KEY_STEP, KEY_STEP)

    def over_keys(step_fn, init):
        def full(c, carry):
            return step_fn(pl.multiple_of(c * KEY_STEP, KEY_STEP), KEY_STEP, c, carry)
        carry = lax.fori_loop(0, n_full, full, init)
        return lax.cond(has_half, lambda cr: step_fn(tail0, HALF_STEP, n_full, cr), lambda cr: cr, carry)

    lane = lax.broadcasted_iota(I32, (n_q, V7X_LANES), 1)
    for hh in range(IDX_HEADS):
        tile = qidx_ref[:, (hh // 2) * V7X_LANES:(hh // 2 + 1) * V7X_LANES].astype(F32)
        keep = (lane < IDX_DIM) if hh % 2 == 0 else (lane >= IDX_DIM)
        qall_scr[hh * n_q:(hh + 1) * n_q, :] = jnp.where(keep, tile, 0.0).astype(BF16)
    for hh in range(N_HEADS):
        qlat_scr[hh * n_q:(hh + 1) * n_q, 0:KV_LORA] = qlat_ref[:, hh * KV_LORA:(hh + 1) * KV_LORA]

    @pl.when(i == 0)
    def _():
        eye = (lax.broadcasted_iota(I32, (n_q, n_q), 0) == lax.broadcasted_iota(I32, (n_q, n_q), 1))
        eye = jnp.where(eye, 1.0, 0.0).astype(BF16)
        for hh in range(N_HEADS):
            qlat_scr[hh * n_q:(hh + 1) * n_q, KV_LORA:KV_LORA + n_q] = eye

    qabs_chunk = (i * n_q + lax.broadcasted_iota(I32, (SCORE_ROWS, n_q), 1)) // CHUNK

    def score_step(first, rows, c, carry):
        for sub in range(rows // SCORE_ROWS):
            r0 = pl.multiple_of(first + sub * SCORE_ROWS, SCORE_ROWS)
            kx = kidx_ref[pl.ds(r0, SCORE_ROWS), :]
            sc = jnp.zeros((SCORE_ROWS, n_q), F32)
            for hp in range(IDX_HEADS // 2):
                lg = _dot_nt(kx, qall_scr[hp * MXU_COLS:(hp + 1) * MXU_COLS, :])
                for h2 in range(2):
                    hh = 2 * hp + h2
                    sc = sc + widx_ref[hh:hh + 1, :] * jnp.maximum(lg[:, h2 * n_q:(h2 + 1) * n_q], 0.0)
            bits = pltpu.bitcast(sc, I32)
            key = bits ^ ((bits >> 31) & 0x7FFFFFFF)
            key = jnp.where(bits == INT_MIN, 0, key)
            kabs_chunk = (r0 + lax.broadcasted_iota(I32, (SCORE_ROWS, n_q), 0)) // CHUNK
            key_scr[pl.ds(r0, SCORE_ROWS), :] = jnp.where(kabs_chunk <= qabs_chunk, key, INT_MIN)
        return carry

    over_keys(score_step, 0)

    def count_rows(pred):
        def body(r0, rows, c, cnt):
            hit = pred(key_scr[pl.ds(r0, rows), :], r0)
            ones = jnp.where(hit, 1, 0).astype(I32)
            return cnt + jnp.sum(ones.reshape(rows // V7X_SUBLANES, V7X_SUBLANES, n_q), axis=0)
        cnt8 = over_keys(body, jnp.zeros((V7X_SUBLANES, n_q), I32))
        return jnp.sum(cnt8, axis=0, keepdims=True)

    def bisect_bits(it, cur, settled, n_bits):
        for j in range(n_bits):
            cand = cur ^ jnp.left_shift(jnp.int32(1), 31 - (it + j))
            cnt = count_rows(lambda k, r0: k >= cand)
            step = jnp.where(cnt >= top_k, cand, cur)
            cur = jnp.where(settled > 0, cur, step)
            settled = jnp.maximum(settled, jnp.where(cnt == top_k, 1, 0))
        return cur, settled

    def open_count(settled):
        return n_q - jnp.sum(settled.astype(F32))

    def unchecked_group(g, state):
        return bisect_bits(g * BISECT_GROUP, *state, BISECT_GROUP)

    def bisect_group(state):
        it, cur, settled, _ = state
        cur, settled = bisect_bits(it, cur, settled, BISECT_GROUP)
        return it + BISECT_GROUP, cur, settled, open_count(settled)

    def bisect_more(state):
        it, _, _, open_lanes = state
        return jnp.logical_and(it < 32, open_lanes > 0.0)

    def search_threshold():
        cur0, settled0 = lax.fori_loop(
            0, BISECT_UNCHECKED // BISECT_GROUP, unchecked_group,
            (jnp.full((1, n_q), INT_MIN, I32), jnp.zeros((1, n_q), I32)))
        _, cur, _, still_open = lax.while_loop(
            bisect_more, bisect_group, (jnp.int32(BISECT_UNCHECKED), cur0, settled0, open_count(settled0)))
        return cur, still_open

    thr, open_lanes = lax.cond((i + 1) * Q_BLOCK > top_k, search_threshold,
                               lambda: (jnp.full((1, n_q), INT_MIN, I32), jnp.float32(0.0)))
    thr = jnp.maximum(thr, INT_MIN + 1)
    all_rows = lambda: jnp.full((1, n_q), seq_len, I32)

    def tie_rows():
        n_gt = count_rows(lambda k, r0: k > thr)
        n_ge = count_rows(lambda k, r0: k >= thr)
        need = top_k - n_gt

        def tie_limit():
            n_bits = max(1, (seq_len - 1).bit_length())

            def bisect_row(it, cur):
                cand = cur + jnp.left_shift(jnp.int32(1), n_bits - 1 - it)

                def pred(k, r0):
                    row = r0 + lax.broadcasted_iota(I32, k.shape, 0)
                    return jnp.logical_and(k == thr, row < cand)
                cnt = count_rows(pred)
                return jnp.where(cnt < need, cand, cur)

            return lax.fori_loop(0, n_bits, bisect_row, jnp.zeros((1, n_q), I32))

        return lax.cond(jnp.max(n_ge.astype(F32)) > top_k, tie_limit, all_rows)

    row_lim = lax.cond(open_lanes > 0.0, tie_rows, all_rows)

    pair = 2 * n_q
    n_pair = N_HEADS // 2
    n_col = N_HEADS * n_q

    def qk_step(r0, rows, c, m8):
        k = key_scr[pl.ds(r0, rows), :]
        row = r0 + lax.broadcasted_iota(I32, (rows, n_q), 0)
        take = jnp.logical_or(k > thr, jnp.logical_and(k == thr, row <= row_lim))
        mask = jnp.where(take, 0.0, NEG_BIG).astype(BF16)
        kvb = jnp.concatenate([ckv_ref[pl.ds(r0, rows), :], mask], axis=1)
        parts = []
        for hp in range(n_pair):
            s = _dot_nt(kvb, qlat_scr[hp * pair:(hp + 1) * pair, :])
            s_scr[pl.ds(r0, rows), hp * pair:(hp + 1) * pair] = s
            parts.append(jnp.max(s.reshape(rows // V7X_SUBLANES, V7X_SUBLANES, pair), axis=0))
        return jnp.maximum(m8, jnp.concatenate(parts, axis=1))

    m8 = over_keys(qk_step, jnp.full((V7X_SUBLANES, n_col), NEG_BIG, F32))
    m_scr[...] = jnp.max(m8, axis=0, keepdims=True)

    acc_scr[...] = jnp.zeros(acc_scr.shape, F32)

    def pv_step(r0, rows, c, carry):
        kvt = ckvt_ref[c, :, 0:rows]
        for hp in range(n_pair):
            cols = slice(hp * pair, (hp + 1) * pair)
            p = jnp.exp2(s_scr[pl.ds(r0, rows), cols] - m_scr[:, cols])
            acc_scr[:, cols] += _dot(kvt, p.astype(BF16))
        return carry

    over_keys(pv_step, 0)

    for hh in range(N_HEADS):
        cols = slice(hh * n_q, (hh + 1) * n_q)
        inv_l = 1.0 / acc_scr[KV_LORA:KV_LORA + 1, cols]
        olat = (acc_scr[0:KV_LORA, cols] * inv_l).astype(BF16)
        ot_scr[hh * V_DIM:(hh + 1) * V_DIM, :] = _dot(wuvt_ref[hh], olat)
    o_ref[...] = ot_scr[...].T.astype(BF16)


def _kernel_c(x_ref, o_ref, ma_ref, gb_ref, wao_ref, wo_ref, mng_ref, w1_ref, w2_ref, fng_ref,
              out_ref, *, ff_step):
    yb = _dot(o_ref[...], wao_ref[...])
    merged = ma_ref[...].astype(F32) + gb_ref[...].astype(F32) * yb
    x1 = x_ref[...] + _dot(merged.astype(BF16), wo_ref[...])
    hm = _rms(x1, mng_ref[...]).astype(BF16)
    acc = x1
    for j in range(D_FF // ff_step):
        h1 = jnp.maximum(_dot(hm, w1_ref[:, j * ff_step:(j + 1) * ff_step]), 0.0)
        acc = acc + _dot((h1 * h1).astype(BF16), w2_ref[j * ff_step:(j + 1) * ff_step, :])
    out_ref[...] = _rms(acc, fng_ref[...])


def _const_spec(shape):
    nd = len(shape)
    return pl.BlockSpec(shape, lambda *_: (0,) * nd, pipeline_mode=pl.Buffered(1))


def _params(semantics):
    return pltpu.CompilerParams(dimension_semantics=semantics,
                                vmem_limit_bytes=V7X_VMEM_LIMIT_BYTES)


def _layer(x, attn_norm_g, w_in, b_gate, dw_w, dw_b, conv_ln_g, conv_ln_b, w_conv_out, q_norm_g,
           kv_norm_g, w_uq, w_uk, w_uv, w_qi, kidx_ln_g, kidx_ln_b, w_attn_out, w_o, mlp_norm_g,
           w_ff1, w_ff2, out_norm_g):
    bsz, seq_len, _ = x.shape
    n_tok = bsz * seq_len
    top_k = min(IDX_TOPK_MAX, seq_len // 4)
    tm_a = min(256, seq_len)
    tm_c = min(512, seq_len)
    assert seq_len % Q_BLOCK == 0 and seq_len % tm_a == 0 and seq_len % tm_c == 0
    assert seq_len % KEY_STEP == 0

    row = lambda v: v.reshape(1, -1).astype(F32)
    x2 = x.reshape(n_tok, D_MODEL)

    o_cq = 2 * D_MODEL
    o_ckv = o_cq + Q_LORA
    o_kidx = o_ckv + KV_LORA
    o_widx = o_kidx + IDX_DIM
    o_gate = o_widx + IDX_HEADS
    wbig = jnp.concatenate([w_in[:, 0:2 * D_MODEL], w_in[:, o_gate:o_gate + 2 * D_MODEL]],
                           axis=1).astype(BF16)
    n_small = o_gate - o_cq
    wsm = jnp.pad(w_in[:, o_cq:o_gate], ((0, 0), (0, 4 * V7X_LANES - n_small))).astype(BF16)

    dww = jnp.broadcast_to(dw_w.astype(F32)[:, None, :], (CONV_KERNEL, V7X_SUBLANES, D_MODEL))
    klng = jnp.pad(kidx_ln_g, (0, V7X_LANES - IDX_DIM)).reshape(1, -1).astype(F32)
    klnb = jnp.pad(kidx_ln_b, (0, V7X_LANES - IDX_DIM)).reshape(1, -1).astype(F32)

    hpg = 4
    wuk_g = w_uk.reshape(N_HEADS // hpg, hpg, HEAD_DIM, KV_LORA)
    eye = jnp.eye(hpg, dtype=w_uk.dtype)
    wukbd = jnp.einsum("ghdc,hk->ghdkc", wuk_g, eye).reshape(
        N_HEADS // hpg, hpg * HEAD_DIM, hpg * KV_LORA).astype(BF16)

    tiles_per_seq = seq_len // tm_a
    tok_spec = lambda tm, n: pl.BlockSpec((tm, n), lambda t: (t, 0))
    a_in = [x2, row(attn_norm_g), wbig, wsm, row(b_gate), dww, row(dw_b), row(conv_ln_g),
            row(conv_ln_b), w_conv_out.astype(BF16), row(q_norm_g), row(kv_norm_g),
            w_uq.astype(BF16), wukbd, w_qi.astype(BF16), klng, klnb]
    a_specs = [tok_spec(tm_a, D_MODEL)] + [_const_spec(a.shape) for a in a_in[1:]]
    n_lat = N_HEADS * KV_LORA
    n_qidx = IDX_HEADS * IDX_DIM
    n_kstep = seq_len // KEY_STEP
    tiles_per_kstep = KEY_STEP // tm_a
    ma, gb, qlat, qidx, ckv, ckvt, kidx, widx_t = pl.pallas_call(
        functools.partial(_kernel_a, tm=tm_a, tiles_per_seq=tiles_per_seq),
        grid=(n_tok // tm_a,),
        in_specs=a_specs,
        out_specs=[tok_spec(tm_a, D_MODEL), tok_spec(tm_a, D_MODEL), tok_spec(tm_a, n_lat),
                   tok_spec(tm_a, n_qidx), tok_spec(tm_a, KV_LORA),
                   pl.BlockSpec((None, None, PV_ROWS, tm_a),
                                lambda t: (t // tiles_per_seq, (t % tiles_per_seq) // tiles_per_kstep, 0,
                                           t % tiles_per_kstep)),
                   tok_spec(tm_a, V7X_LANES),
                   pl.BlockSpec((None, IDX_HEADS, tm_a), lambda t: (t // tiles_per_seq, 0, t % tiles_per_seq))],
        out_shape=[jax.ShapeDtypeStruct((n_tok, D_MODEL), BF16),
                   jax.ShapeDtypeStruct((n_tok, D_MODEL), BF16),
                   jax.ShapeDtypeStruct((n_tok, n_lat), BF16),
                   jax.ShapeDtypeStruct((n_tok, n_qidx), BF16),
                   jax.ShapeDtypeStruct((n_tok, KV_LORA), BF16),
                   jax.ShapeDtypeStruct((bsz, n_kstep, PV_ROWS, KEY_STEP), BF16),
                   jax.ShapeDtypeStruct((n_tok, V7X_LANES), BF16),
                   jax.ShapeDtypeStruct((bsz, IDX_HEADS, seq_len), F32)],
        scratch_shapes=[pltpu.VMEM((V7X_SUBLANES, D_MODEL // V7X_LANES, CONV_HALO + tm_a, V7X_LANES), F32),
                        pltpu.VMEM((tm_a, D_MODEL), F32),
                        pltpu.VMEM((tm_a, D_MODEL), BF16),
                        pltpu.VMEM((tm_a, D_MODEL), F32)],
        compiler_params=_params(("arbitrary",)),
        name="block_in",
    )(*a_in)

    ckv = ckv.reshape(bsz, seq_len, KV_LORA)
    kidx = kidx.reshape(bsz, seq_len, V7X_LANES)
    wuvt = jnp.swapaxes(w_uv, 1, 2).astype(BF16)

    n_qb = seq_len // Q_BLOCK
    seq_spec = lambda r, c: pl.BlockSpec((None, r, c), lambda b, i: (b, 0, 0))
    o = pl.pallas_call(
        functools.partial(_kernel_b, seq_len=seq_len, top_k=top_k),
        grid=(bsz, n_qb),
        in_specs=[seq_spec(seq_len, V7X_LANES), seq_spec(seq_len, KV_LORA),
                  pl.BlockSpec((None, n_kstep, PV_ROWS, KEY_STEP), lambda b, i: (b, 0, 0, 0)),
                  pl.BlockSpec((None, IDX_HEADS, Q_BLOCK), lambda b, i: (b, 0, i)),
                  pl.BlockSpec((Q_BLOCK, n_qidx), lambda b, i: (b * n_qb + i, 0)),
                  pl.BlockSpec((Q_BLOCK, n_lat), lambda b, i: (b * n_qb + i, 0)),
                  pl.BlockSpec(wuvt.shape, lambda b, i: (0, 0, 0))],
        out_specs=pl.BlockSpec((Q_BLOCK, N_HEADS * V_DIM), lambda b, i: (b * n_qb + i, 0)),
        out_shape=jax.ShapeDtypeStruct((n_tok, N_HEADS * V_DIM), BF16),
        scratch_shapes=[pltpu.VMEM((IDX_HEADS * Q_BLOCK, V7X_LANES), BF16),
                        pltpu.VMEM((N_HEADS * Q_BLOCK, KV_LORA + Q_BLOCK), BF16),
                        pltpu.VMEM((seq_len, Q_BLOCK), I32),
                        pltpu.VMEM((seq_len, N_HEADS * Q_BLOCK), F32),
                        pltpu.VMEM((1, N_HEADS * Q_BLOCK), F32),
                        pltpu.VMEM((PV_ROWS, N_HEADS * Q_BLOCK), F32),
                        pltpu.VMEM((N_HEADS * V_DIM, Q_BLOCK), F32)],
        compiler_params=_params(("arbitrary", "arbitrary")),
        name="sparse_attn",
    )(kidx, ckv, ckvt, widx_t, qidx, qlat, wuvt)

    c_in = [x2, o, ma, gb, w_attn_out.astype(BF16), w_o.astype(BF16), row(mlp_norm_g),
            w_ff1.astype(BF16), w_ff2.astype(BF16), row(out_norm_g)]
    c_specs = [tok_spec(tm_c, D_MODEL)] * 4 + [_const_spec(a.shape) for a in c_in[4:]]
    out = pl.pallas_call(
        functools.partial(_kernel_c, ff_step=D_MODEL),
        grid=(n_tok // tm_c,),
        in_specs=c_specs,
        out_specs=tok_spec(tm_c, D_MODEL),
        out_shape=jax.ShapeDtypeStruct((n_tok, D_MODEL), F32),
        compiler_params=_params(("arbitrary",)),
        name="block_out",
    )(*c_in)
    return out.reshape(bsz, seq_len, D_MODEL)


def kernel(x, attn_norm_g, w_in, b_gate, dw_w, dw_b, conv_ln_g, conv_ln_b, w_conv_out, q_norm_g,
           kv_norm_g, w_uq, w_uk, w_uv, w_qi, kidx_ln_g, kidx_ln_b, w_attn_out, w_o, mlp_norm_g,
           w_ff1, w_ff2, final_norm_g):
    assert attn_norm_g.shape[0] == 1, "one layer: the final RMSNorm is fused into its output kernel"
    return _layer(x, attn_norm_g[0], w_in[0], b_gate[0], dw_w[0], dw_b[0], conv_ln_g[0],
                  conv_ln_b[0], w_conv_out[0], q_norm_g[0], kv_norm_g[0], w_uq[0], w_uk[0], w_uv[0],
                  w_qi[0], kidx_ln_g[0], kidx_ln_b[0], w_attn_out[0], w_o[0], mlp_norm_g[0],
                  w_ff1[0], w_ff2[0], final_norm_g)
```
